```python
import math, functools
import jax, jax.numpy as jnp
from jax import lax
import numpy as np

D_MODEL = 1024
BATCH = 16
SEQ = 2048
DEPTH = 2
DEC_BATCH = 16
DEC_SEQ = 16
PAST_LEN = 2048

CHUNK = 64
HEAD_DIM = 64
N_MIXERS = 2
A_HEADS = 16
A_LEFT_CHUNKS = 8
A_REL_CLIP = 128
B_Q_HEADS = 16
B_KV_HEADS = 4
B_WINDOW = 128
B_LEFT_CHUNKS = B_WINDOW // CHUNK
D_FF = 2816
ROPE_THETA = 10000.0
LN_EPS = 1e-5
DEEPNORM_ALPHA = (2.0 * DEPTH) ** 0.25
DEEPNORM_BETA = (8.0 * DEPTH) ** -0.25
N_A_LAYERS = (DEPTH + 1) // 2
N_B_LAYERS = DEPTH // 2

kernel_name = "streaming_chunkband_sinkswa_macaron_step"


def _layer_norm(x, g, b):
    xf = x.astype(jnp.float32)
    mu = jnp.mean(xf, axis=-1, keepdims=True)
    var = jnp.mean(jnp.square(xf - mu), axis=-1, keepdims=True)
    y = (xf - mu) * lax.rsqrt(var + LN_EPS) * g.astype(jnp.float32) + b.astype(jnp.float32)
    return y.astype(x.dtype)


def _post_norm(x, sub, g, b):
    return _layer_norm(DEEPNORM_ALPHA * x + sub, g, b)


def _swiglu(x, w_in, w_down):
    gate, up = jnp.split(x @ w_in, 2, axis=-1)
    return (jax.nn.silu(gate) * up) @ w_down


def _rope(x, pos):
    half = HEAD_DIM // 2
    inv = ROPE_THETA ** (-jnp.arange(half, dtype=jnp.float32) / half)
    ang = pos.astype(jnp.float32)[:, None] * inv[None, :]
    cos = jnp.cos(ang)[None, :, None, :]
    sin = jnp.sin(ang)[None, :, None, :]
    xf = x.astype(jnp.float32)
    x1, x2 = xf[..., :half], xf[..., half:]
    return jnp.concatenate([x1 * cos - x2 * sin, x2 * cos + x1 * sin], axis=-1).astype(x.dtype)


def _split_qkv(h, hq, hkv):
    bsz, t, _ = h.shape
    q, k, v = jnp.split(h, [hq * HEAD_DIM, (hq + hkv) * HEAD_DIM], axis=-1)
    return (q.reshape(bsz, t, hq, HEAD_DIM), k.reshape(bsz, t, hkv, HEAD_DIM),
            v.reshape(bsz, t, hkv, HEAD_DIM))


def _rel_bias(table, q_off, k_off):
    rel = jnp.clip(q_off[:, None] - k_off[None, :], -A_REL_CLIP, A_REL_CLIP) + A_REL_CLIP
    return table.astype(jnp.float32)[:, rel][:, None]


def _attend(q, k, v, valid, bias, sinks):
    s = jnp.einsum("bqhgd,bkhd->bhgqk", q, k, preferred_element_type=jnp.float32) * (HEAD_DIM ** -0.5)
    if bias is not None:
        s = s + bias
    if valid is not None:
        s = jnp.where(valid, s, -jnp.inf)
    if sinks is None:
        p = jax.nn.softmax(s, axis=-1)
    else:
        sk = sinks.astype(jnp.float32)[None, :, :, None, None]
        m = jnp.maximum(jnp.max(s, axis=-1, keepdims=True), sk)
        e = jnp.exp(s - m)
        p = e / (jnp.sum(e, axis=-1, keepdims=True) + jnp.exp(sk - m))
    return jnp.einsum("bhgqk,bkhd->bqhgd", p.astype(v.dtype), v)


def _band_attention(q, k, v, n_left, bias, sinks):
    bsz, s_len, hkv, g, hd = q.shape
    n_chunks = s_len // CHUNK
    pad = n_left * CHUNK
    band = pad + CHUNK
    kp = jnp.pad(k, ((0, 0), (pad, 0), (0, 0), (0, 0)))
    vp = jnp.pad(v, ((0, 0), (pad, 0), (0, 0), (0, 0)))
    qc = jnp.moveaxis(q.reshape(bsz, n_chunks, CHUNK, hkv, g, hd), 1, 0)
    offs = jnp.arange(band) - pad

    def one_chunk(args):
        c, qb = args
        start = c * CHUNK
        kb = lax.dynamic_slice_in_dim(kp, start, band, axis=1)
        vb = lax.dynamic_slice_in_dim(vp, start, band, axis=1)
        valid = (start + offs >= 0)[None, :]
        return _attend(qb, kb, vb, valid, bias, sinks)

    out = lax.map(one_chunk, (jnp.arange(n_chunks), qc))
    return jnp.moveaxis(out, 0, 1).reshape(bsz, s_len, hkv * g * hd)


def _mixer_a_prompt(x, w_qkv, w_o, table):
    bsz, s_len, _ = x.shape
    q, k, v = _split_qkv(x @ w_qkv, A_HEADS, A_HEADS)
    pad = A_LEFT_CHUNKS * CHUNK
    bias = _rel_bias(table, jnp.arange(CHUNK), jnp.arange(pad + CHUNK) - pad)
    o = _band_attention(q.reshape(bsz, s_len, A_HEADS, 1, HEAD_DIM), k, v, A_LEFT_CHUNKS, bias, None)
    keep = min(pad, s_len)
    return o @ w_o, k[:, s_len - keep:], v[:, s_len - keep:]


def _mixer_a_step(x, w_qkv, w_o, table, k_cache, v_cache):
    bsz, t, _ = x.shape
    q, k, v = _split_qkv(x @ w_qkv, A_HEADS, A_HEADS)
    lc = k_cache.shape[1]
    kk = jnp.concatenate([k_cache, k], axis=1)
    vv = jnp.concatenate([v_cache, v], axis=1)
    bias = _rel_bias(table, jnp.arange(t), jnp.arange(lc + t) - lc)
    o = _attend(q.reshape(bsz, t, A_HEADS, 1, HEAD_DIM), kk, vv, None, bias, None)
    return o.reshape(bsz, t, A_HEADS * HEAD_DIM) @ w_o, k, v


def _mixer_b_prompt(x, w_qkv, w_o, sinks):
    bsz, s_len, _ = x.shape
    q, k, v = _split_qkv(x @ w_qkv, B_Q_HEADS, B_KV_HEADS)
    pos = jnp.arange(s_len)
    q, k = _rope(q, pos), _rope(k, pos)
    g = B_Q_HEADS // B_KV_HEADS
    o = _band_attention(q.reshape(bsz, s_len, B_KV_HEADS, g, HEAD_DIM), k, v, B_LEFT_CHUNKS,
                        None, sinks.reshape(B_KV_HEADS, g))
    keep = min(B_WINDOW, s_len)
    return o @ w_o, k[:, s_len - keep:], v[:, s_len - keep:]


def _mixer_b_step(x, w_qkv, w_o, sinks, k_cache, v_cache):
    bsz, t, _ = x.shape
    q, k, v = _split_qkv(x @ w_qkv, B_Q_HEADS, B_KV_HEADS)
    pos = PAST_LEN + jnp.arange(t)
    q, k = _rope(q, pos), _rope(k, pos)
    g = B_Q_HEADS // B_KV_HEADS
    kk = jnp.concatenate([k_cache, k], axis=1)
    vv = jnp.concatenate([v_cache, v], axis=1)
    o = _attend(q.reshape(bsz, t, B_KV_HEADS, g, HEAD_DIM), kk, vv, None, None,
                sinks.reshape(B_KV_HEADS, g))
    return o.reshape(bsz, t, B_Q_HEADS * HEAD_DIM) @ w_o, k, v


def _macaron_layer(x, i, mixer, ln_g, ln_b, w_ffn_in, w_ffn_down):
    x = _post_norm(x, 0.5 * _swiglu(x, w_ffn_in[i, 0], w_ffn_down[i, 0]), ln_g[i, 0], ln_b[i, 0])
    m, k_new, v_new = mixer(x)
    x = _post_norm(x, m, ln_g[i, 1], ln_b[i, 1])
    x = _post_norm(x, 0.5 * _swiglu(x, w_ffn_in[i, 1], w_ffn_down[i, 1]), ln_g[i, 2], ln_b[i, 2])
    return x, k_new, v_new


def setup_inputs(seed: int = 0) -> dict:
    key = jax.random.key(seed)
    ks = jax.random.split(key, 16)
    f32 = jnp.float32

    def nrm(k, shape, scale):
        return scale * jax.random.normal(k, shape, f32)

    la = min(A_LEFT_CHUNKS * CHUNK, PAST_LEN)
    lb = min(B_WINDOW, PAST_LEN)
    qkv_b_cols = (B_Q_HEADS + 2 * B_KV_HEADS) * HEAD_DIM
    return {
        "x_prompt": nrm(ks[0], (BATCH, SEQ, D_MODEL), 1.0),
        "x_sample": nrm(ks[1], (DEC_BATCH, DEC_SEQ, D_MODEL), 1.0),
        "cache_a_k": nrm(ks[2], (N_A_LAYERS, DEC_BATCH, la, A_HEADS, HEAD_DIM), 1.0),
        "cache_a_v": nrm(ks[3], (N_A_LAYERS, DEC_BATCH, la, A_HEADS, HEAD_DIM), 1.0),
        "cache_b_k": nrm(ks[4], (N_B_LAYERS, DEC_BATCH, lb, B_KV_HEADS, HEAD_DIM), 1.0),
        "cache_b_v": nrm(ks[5], (N_B_LAYERS, DEC_BATCH, lb, B_KV_HEADS, HEAD_DIM), 1.0),
        "ln_g": 1.0 + nrm(ks[6], (DEPTH, 3, D_MODEL), 0.02),
        "ln_b": nrm(ks[7], (DEPTH, 3, D_MODEL), 0.02),
        "w_ffn_in": nrm(ks[8], (DEPTH, 2, D_MODEL, 2 * D_FF), D_MODEL ** -0.5),
        "w_ffn_down": nrm(ks[9], (DEPTH, 2, D_FF, D_MODEL), DEEPNORM_BETA * D_FF ** -0.5),
        "w_qkv_a": nrm(ks[10], (N_A_LAYERS, D_MODEL, 3 * A_HEADS * HEAD_DIM), D_MODEL ** -0.5),
        "w_o_a": nrm(ks[11], (N_A_LAYERS, A_HEADS * HEAD_DIM, D_MODEL),
                     DEEPNORM_BETA * (A_HEADS * HEAD_DIM) ** -0.5),
        "rel_bias_a": nrm(ks[12], (N_A_LAYERS, A_HEADS, 2 * A_REL_CLIP + 1), 0.5),
        "w_qkv_b": nrm(ks[13], (N_B_LAYERS, D_MODEL, qkv_b_cols), D_MODEL ** -0.5),
        "w_o_b": nrm(ks[14], (N_B_LAYERS, B_Q_HEADS * HEAD_DIM, D_MODEL),
                     DEEPNORM_BETA * (B_Q_HEADS * HEAD_DIM) ** -0.5),
        "sinks_b": nrm(ks[15], (N_B_LAYERS, B_Q_HEADS), 1.0),
    }


def reference(x_prompt, x_sample, cache_a_k, cache_a_v, cache_b_k, cache_b_v,
              ln_g, ln_b, w_ffn_in, w_ffn_down, w_qkv_a, w_o_a, rel_bias_a,
              w_qkv_b, w_o_b, sinks_b):
    xp, xs = x_prompt, x_sample
    ak_p, av_p, bk_p, bv_p = [], [], [], []
    ak_s, av_s, bk_s, bv_s = [], [], [], []
    for i in range(DEPTH):
        j = i // N_MIXERS
        if i % N_MIXERS == 0:
            mix_p = functools.partial(_mixer_a_prompt, w_qkv=w_qkv_a[j], w_o=w_o_a[j], table=rel_bias_a[j])
            mix_s = functools.partial(_mixer_a_step, w_qkv=w_qkv_a[j], w_o=w_o_a[j], table=rel_bias_a[j],
                                      k_cache=cache_a_k[j], v_cache=cache_a_v[j])
        else:
            mix_p = functools.partial(_mixer_b_prompt, w_qkv=w_qkv_b[j], w_o=w_o_b[j], sinks=sinks_b[j])
            mix_s = functools.partial(_mixer_b_step, w_qkv=w_qkv_b[j], w_o=w_o_b[j], sinks=sinks_b[j],
                                      k_cache=cache_b_k[j], v_cache=cache_b_v[j])
        xp, kp_new, vp_new = _macaron_layer(xp, i, mix_p, ln_g, ln_b, w_ffn_in, w_ffn_down)
        xs, ks_new, vs_new = _macaron_layer(xs, i, mix_s, ln_g, ln_b, w_ffn_in, w_ffn_down)
        if i % N_MIXERS == 0:
            ak_p.append(kp_new); av_p.append(vp_new); ak_s.append(ks_new); av_s.append(vs_new)
        else:
            bk_p.append(kp_new); bv_p.append(vp_new); bk_s.append(ks_new); bv_s.append(vs_new)
    return (xp, xs,
            jnp.stack(ak_p), jnp.stack(av_p), jnp.stack(bk_p), jnp.stack(bv_p),
            jnp.stack(ak_s), jnp.stack(av_s), jnp.stack(bk_s), jnp.stack(bv_s))
```

```python
import functools

import jax
import jax.numpy as jnp
from jax import lax
from jax.experimental import pallas as pl
from jax.experimental.pallas import tpu as pltpu

F32 = jnp.float32
BF16 = jnp.bfloat16

DEPTH = 2
CHUNK = 64
HEAD_DIM = 64
A_LEFT_CHUNKS = 8
A_REL_CLIP = 128
B_LEFT_CHUNKS = 2
PAST_LEN = 2048
ROPE_THETA = 10000.0
LN_EPS = 1e-5
DEEPNORM_ALPHA = (2.0 * DEPTH) ** 0.25
QK_SCALE = HEAD_DIM ** -0.5

LANES = 128
PAIR = 2 * CHUNK
MXU_N = 256
V7X_VMEM_LIMIT_BYTES = 56 * 1024 * 1024

_NT = (((1,), (1,)), ((), ()))


def _params(n_axes):
    return pltpu.CompilerParams(
        dimension_semantics=("arbitrary",) * n_axes,
        vmem_limit_bytes=V7X_VMEM_LIMIT_BYTES)


def _resident(shape):
    nd = len(shape)
    return pl.BlockSpec(shape, lambda *_: (0,) * nd, pipeline_mode=pl.Buffered(1))


def _layer_norm_rows(y, g, b):
    mu = jnp.mean(y, axis=-1, keepdims=True)
    d = y - mu
    var = jnp.mean(d * d, axis=-1, keepdims=True)
    return d * lax.rsqrt(var + LN_EPS) * g + b


def _lo_lanes():
    return lax.broadcasted_iota(jnp.int32, (1, LANES), 1) < HEAD_DIM


def _ffn_ln_kernel(x_ref, win_ref, wdown_ref, g_ref, b_ref, o_ref, *, d_ff):
    x = x_ref[...]
    xb = x.astype(BF16)
    gate = jnp.dot(xb, win_ref[:, :d_ff], preferred_element_type=F32)
    up = jnp.dot(xb, win_ref[:, d_ff:], preferred_element_type=F32)
    act = (gate * jax.nn.sigmoid(gate)) * up
    y = jnp.dot(act.astype(BF16), wdown_ref[...], preferred_element_type=F32)
    o_ref[...] = _layer_norm_rows(DEEPNORM_ALPHA * x + 0.5 * y, g_ref[...], b_ref[...])


def _ffn_ln(x, w_in, w_down, g, b, *, tm):
    n, d = x.shape
    d_ff = w_down.shape[0]
    assert n % tm == 0
    return pl.pallas_call(
        functools.partial(_ffn_ln_kernel, d_ff=d_ff),
        out_shape=jax.ShapeDtypeStruct((n, d), F32),
        grid=(n // tm,),
        in_specs=[
            pl.BlockSpec((tm, d), lambda i: (i, 0)),
            _resident((d, 2 * d_ff)),
            _resident((d_ff, d)),
            _resident((1, d)),
            _resident((1, d)),
        ],
        out_specs=pl.BlockSpec((tm, d), lambda i: (i, 0)),
        compiler_params=_params(1),
        name="ffn_ln",
    )(x, w_in, w_down, g, b)


def _softmax_pv(s_parts, v_parts, sink=None):
    m = jnp.max(s_parts[0], axis=1, keepdims=True)
    for s in s_parts[1:]:
        m = jnp.maximum(m, jnp.max(s, axis=1, keepdims=True))
    if sink is not None:
        m = jnp.maximum(m, sink)
    denom = None
    pv = None
    for s, v in zip(s_parts, v_parts):
        e = jnp.exp(s - m)
        part_l = jnp.sum(e, axis=1, keepdims=True)
        part_pv = jnp.dot(e.astype(BF16), v, preferred_element_type=F32)
        denom = part_l if denom is None else denom + part_l
        pv = part_pv if pv is None else pv + part_pv
    if sink is not None:
        denom = denom + jnp.exp(sink - m)
    return pv, denom


def _split_heads(q2, lo):
    zero = jnp.zeros_like(q2)
    return jnp.concatenate([jnp.where(lo, q2, zero), jnp.where(lo, zero, q2)], axis=0)


def _mix_a_prompt_kernel(x_ref, wqkv_ref, wo_ref, bias_ref, g_ref, b_ref,
                         o_ref, kc_ref, vc_ref,
                         q_s, k_s, v_s, att_s, *, tm, n_blk):
    t = pl.program_id(1)
    x = x_ref[...]
    xb = x.astype(BF16)
    row0 = pl.multiple_of(t * tm, tm)
    d_attn = n_blk * LANES
    for j in range(3 * d_attn // MXU_N):
        r = jnp.dot(xb, wqkv_ref[:, j * MXU_N:(j + 1) * MXU_N], preferred_element_type=F32)
        sec, jj = divmod(j, d_attn // MXU_N)
        for u in range(MXU_N // LANES):
            blk = r[:, u * LANES:(u + 1) * LANES]
            hp = jj * (MXU_N // LANES) + u
            if sec == 0:
                q_s[hp] = (blk * QK_SCALE).astype(BF16)
            elif sec == 1:
                k_s[hp, pl.ds(row0, tm), :] = blk.astype(BF16)
                kc_ref[0, :, hp * LANES:(hp + 1) * LANES] = blk
            else:
                v_s[hp, pl.ds(row0, tm), :] = blk.astype(BF16)
                vc_ref[0, :, hp * LANES:(hp + 1) * LANES] = blk

    lo = _lo_lanes()
    win = (A_LEFT_CHUNKS + 2) * CHUNK

    def unit(hp, cp, kstart, nkeys):
        q2 = q_s[hp, cp * PAIR:(cp + 1) * PAIR, :]
        kw = k_s[hp, pl.ds(kstart, nkeys), :]
        vw = v_s[hp, pl.ds(kstart, nkeys), :]
        s = lax.dot_general(_split_heads(q2, lo), kw, _NT, preferred_element_type=F32)
        s = s + bias_ref[hp, :, win - nkeys:]
        pv, denom = _softmax_pv([s], [vw])
        o = pv / denom
        att_s[hp, cp * PAIR:(cp + 1) * PAIR, :] = jnp.where(lo, o[:PAIR], o[PAIR:]).astype(BF16)

    n_cp = tm // PAIR

    @pl.when(t == 0)
    def _():
        def body(hp, c):
            for cp in range(n_cp):
                unit(hp, cp, 0, min(win, (cp + 1) * PAIR))
            return c
        lax.fori_loop(0, n_blk, body, 0)

    @pl.when(t > 0)
    def _():
        def body(hp, c):
            for cp in range(n_cp):
                kstart = pl.multiple_of(row0 + (cp + 1) * PAIR - win, PAIR)
                unit(hp, cp, kstart, win)
            return c
        lax.fori_loop(0, n_blk, body, 0)

    att = jnp.concatenate([att_s[hp] for hp in range(n_blk)], axis=1)
    mix = jnp.dot(att, wo_ref[...], preferred_element_type=F32)
    o_ref[...] = _layer_norm_rows(DEEPNORM_ALPHA * x + mix, g_ref[...], b_ref[...])


def _mix_a_prompt(x, w_qkv, w_o, bias2, g, b, *, bsz, seq, tm):
    n, d = x.shape
    d_attn = w_o.shape[0]
    n_blk = d_attn // LANES
    keep = min(A_LEFT_CHUNKS * CHUNK, seq)
    assert seq % tm == 0 and keep == tm and tm >= A_LEFT_CHUNKS * CHUNK and tm % PAIR == 0
    nt = seq // tm
    return pl.pallas_call(
        functools.partial(_mix_a_prompt_kernel, tm=tm, n_blk=n_blk),
        out_shape=(jax.ShapeDtypeStruct((n, d), F32),
                   jax.ShapeDtypeStruct((bsz, keep, d_attn), F32),
                   jax.ShapeDtypeStruct((bsz, keep, d_attn), F32)),
        grid=(bsz, nt),
        in_specs=[
            pl.BlockSpec((tm, d), lambda bi, t: (bi * nt + t, 0)),
            _resident(w_qkv.shape),
            _resident(w_o.shape),
            _resident(bias2.shape),
            _resident((1, d)),
            _resident((1, d)),
        ],
        out_specs=(pl.BlockSpec((tm, d), lambda bi, t: (bi * nt + t, 0)),
                   pl.BlockSpec((1, keep, d_attn), lambda bi, t: (bi, 0, 0)),
                   pl.BlockSpec((1, keep, d_attn), lambda bi, t: (bi, 0, 0))),
        scratch_shapes=[
            pltpu.VMEM((n_blk, tm, LANES), BF16),
            pltpu.VMEM((n_blk, seq, LANES), BF16),
            pltpu.VMEM((n_blk, seq, LANES), BF16),
            pltpu.VMEM((n_blk, tm, LANES), BF16),
        ],
        compiler_params=_params(2),
        name="mix_a_prompt",
    )(x, w_qkv, w_o, bias2, g, b)


def _mix_a_step_kernel(x_ref, wqkv_ref, wo_ref, bias_ref, kc_ref, vc_ref, g_ref, b_ref,
                       o_ref, kn_ref, vn_ref, qkv_s, att_s, *, t_new, n_blk):
    bi = pl.program_id(0)
    d_attn = n_blk * LANES
    lc = kc_ref.shape[1]

    @pl.when(bi == 0)
    def _():
        qkv = jnp.dot(x_ref[...].astype(BF16), wqkv_ref[...], preferred_element_type=F32)
        qkv_s[...] = qkv
        kn_ref[...] = qkv[:, d_attn:2 * d_attn]
        vn_ref[...] = qkv[:, 2 * d_attn:]

    lo = _lo_lanes()
    rows = pl.ds(pl.multiple_of(bi * t_new, t_new), t_new)
    for hp in range(n_blk):
        cols = slice(hp * LANES, (hp + 1) * LANES)
        q2 = (qkv_s[rows, cols] * QK_SCALE).astype(BF16)
        k_old = kc_ref[0, :, cols].astype(BF16)
        v_old = vc_ref[0, :, cols].astype(BF16)
        k_new = qkv_s[rows, d_attn + hp * LANES:d_attn + (hp + 1) * LANES].astype(BF16)
        v_new = qkv_s[rows, 2 * d_attn + hp * LANES:2 * d_attn + (hp + 1) * LANES].astype(BF16)
        qq = _split_heads(q2, lo)
        bias = jnp.concatenate([bias_ref[hp, 0:t_new, :], bias_ref[hp, PAIR:PAIR + t_new, :]], axis=0)
        s_old = lax.dot_general(qq, k_old, _NT, preferred_element_type=F32) + bias[:, :lc]
        s_new = lax.dot_general(qq, k_new, _NT, preferred_element_type=F32) + bias[:, lc:lc + t_new]
        pv, denom = _softmax_pv([s_old, s_new], [v_old, v_new])
        o = pv / denom
        att_s[rows, cols] = jnp.where(lo, o[:t_new], o[t_new:]).astype(BF16)

    @pl.when(bi == pl.num_programs(0) - 1)
    def _():
        mix = jnp.dot(att_s[...], wo_ref[...], preferred_element_type=F32)
        o_ref[...] = _layer_norm_rows(DEEPNORM_ALPHA * x_ref[...] + mix, g_ref[...], b_ref[...])


def _mix_a_step(x, w_qkv, w_o, bias2, k_cache, v_cache, g, b, *, bsz, t_new):
    n, d = x.shape
    d_attn = w_o.shape[0]
    n_blk = d_attn // LANES
    lc = k_cache.shape[1]
    assert lc == A_LEFT_CHUNKS * CHUNK and t_new <= CHUNK and t_new % 16 == 0
    full = lambda bi: (0, 0)
    return pl.pallas_call(
        functools.partial(_mix_a_step_kernel, t_new=t_new, n_blk=n_blk),
        out_shape=(jax.ShapeDtypeStruct((n, d), F32),
                   jax.ShapeDtypeStruct((n, d_attn), F32),
                   jax.ShapeDtypeStruct((n, d_attn), F32)),
        grid=(bsz,),
        in_specs=[
            _resident((n, d)),
            _resident(w_qkv.shape),
            _resident(w_o.shape),
            _resident(bias2.shape),
            pl.BlockSpec((1, lc, d_attn), lambda bi: (bi, 0, 0)),
            pl.BlockSpec((1, lc, d_attn), lambda bi: (bi, 0, 0)),
            _resident((1, d)),
            _resident((1, d)),
        ],
        out_specs=(pl.BlockSpec((n, d), full),
                   pl.BlockSpec((n, d_attn), full),
                   pl.BlockSpec((n, d_attn), full)),
        scratch_shapes=[
            pltpu.VMEM((n, 3 * d_attn), F32),
            pltpu.VMEM((n, d_attn), BF16),
        ],
        compiler_params=_params(1),
        name="mix_a_step",
    )(x, w_qkv, w_o, bias2, k_cache, v_cache, g, b)


def _rope_block(blk, cos, sin_up, sin_dn):
    half = HEAD_DIM // 2
    return blk * cos + pltpu.roll(blk, half, 1) * sin_up + pltpu.roll(blk, LANES - half, 1) * sin_dn


def _stack_q_for_kv(blk, j, lo):
    zero = jnp.zeros_like(blk)
    rolled = pltpu.roll(blk, HEAD_DIM, 1)
    if (j // 2) % 2 == 0:
        return jnp.where(lo, blk, zero), jnp.where(lo, rolled, zero)
    return jnp.where(lo, zero, rolled), jnp.where(lo, zero, blk)


def _unstack_o(o, i_pair, rows, lo):
    i0, i1 = 2 * i_pair, 2 * i_pair + 1
    o0 = o[i0 * rows:(i0 + 1) * rows]
    o1 = o[i1 * rows:(i1 + 1) * rows]
    if i0 // 4 == 0:
        o1 = pltpu.roll(o1, HEAD_DIM, 1)
    else:
        o0 = pltpu.roll(o0, HEAD_DIM, 1)
    return jnp.where(lo, o0, o1)


def _mix_b_prompt_kernel(x_ref, wqkv_ref, wo_ref, cos_ref, sup_ref, sdn_ref, mask_ref, sink_ref,
                         g_ref, b_ref, o_ref, kc_ref, vc_ref,
                         q_s, k_s, v_s, att_s, *, tm, n_qblk, n_kvblk, keep):
    t = pl.program_id(1)
    x = x_ref[...]
    xb = x.astype(BF16)
    row0 = pl.multiple_of(t * tm, tm)
    lo = _lo_lanes()
    cos, sup, sdn = cos_ref[...], sup_ref[...], sdn_ref[...]
    dq = n_qblk * LANES
    dkv = n_kvblk * LANES
    for j in range((dq + 2 * dkv) // MXU_N):
        r = jnp.dot(xb, wqkv_ref[:, j * MXU_N:(j + 1) * MXU_N], preferred_element_type=F32)
        for u in range(MXU_N // LANES):
            blk = r[:, u * LANES:(u + 1) * LANES]
            col = j * MXU_N + u * LANES
            if col < dq:
                jq = col // LANES
                h0, h1 = _stack_q_for_kv(_rope_block(blk, cos, sup, sdn) * QK_SCALE, jq, lo)
                q_s[2 * jq] = h0.astype(BF16)
                q_s[2 * jq + 1] = h1.astype(BF16)
            elif col < dq + dkv:
                kb = (col - dq) // LANES
                kr = _rope_block(blk, cos, sup, sdn)
                k_s[kb, pl.ds(row0, tm), :] = kr.astype(BF16)
                kc_ref[0, :, kb * LANES:(kb + 1) * LANES] = kr[tm - keep:]
            else:
                kb = (col - dq - dkv) // LANES
                v_s[kb, pl.ds(row0, tm), :] = blk.astype(BF16)
                vc_ref[0, :, kb * LANES:(kb + 1) * LANES] = blk[tm - keep:]

    win = (B_LEFT_CHUNKS + 2) * CHUNK
    heads_per_kvblk = 2 * n_qblk // n_kvblk

    def unit(kb, cp, kstart, nkeys):
        q = q_s[kb * heads_per_kvblk:(kb + 1) * heads_per_kvblk, cp * PAIR:(cp + 1) * PAIR, :]
        q = q.reshape(heads_per_kvblk * PAIR, LANES)
        kw = k_s[kb, pl.ds(kstart, nkeys), :]
        vw = v_s[kb, pl.ds(kstart, nkeys), :]
        s = lax.dot_general(q, kw, _NT, preferred_element_type=F32)
        s = s + mask_ref[:, win - nkeys:]
        pv, denom = _softmax_pv([s], [vw], sink=sink_ref[kb])
        o = pv / denom
        for a in range(heads_per_kvblk // 2):
            jq = kb * (heads_per_kvblk // 2) + a
            att_s[jq, cp * PAIR:(cp + 1) * PAIR, :] = _unstack_o(o, a, PAIR, lo).astype(BF16)

    for kb in range(n_kvblk):
        @pl.when(t == 0)
        def _():
            unit(kb, 0, 0, PAIR)

        @pl.when(t > 0)
        def _():
            unit(kb, 0, pl.multiple_of(row0 + PAIR - win, PAIR), win)

        for cp in range(1, tm // PAIR):
            unit(kb, cp, pl.multiple_of(row0 + (cp + 1) * PAIR - win, PAIR), win)

    att = jnp.concatenate([att_s[jq] for jq in range(n_qblk)], axis=1)
    mix = jnp.dot(att, wo_ref[...], preferred_element_type=F32)
    o_ref[...] = _layer_norm_rows(DEEPNORM_ALPHA * x + mix, g_ref[...], b_ref[...])


def _mix_b_prompt(x, w_qkv, w_o, rope, mask, sink_col, g, b, *, bsz, seq, tm, keep):
    n, d = x.shape
    dq = w_o.shape[0]
    dkv = (w_qkv.shape[1] - dq) // 2
    n_qblk, n_kvblk = dq // LANES, dkv // LANES
    assert seq % tm == 0 and keep <= tm and tm % PAIR == 0 and tm >= (B_LEFT_CHUNKS + 2) * CHUNK
    assert (2 * n_qblk) % n_kvblk == 0 and (2 * n_qblk // n_kvblk) % 4 == 0
    nt = seq // tm
    cos, sup, sdn = rope
    rope_spec = pl.BlockSpec((tm, LANES), lambda bi, t: (t, 0))
    return pl.pallas_call(
        functools.partial(_mix_b_prompt_kernel, tm=tm, n_qblk=n_qblk, n_kvblk=n_kvblk, keep=keep),
        out_shape=(jax.ShapeDtypeStruct((n, d), F32),
                   jax.ShapeDtypeStruct((bsz, keep, dkv), F32),
                   jax.ShapeDtypeStruct((bsz, keep, dkv), F32)),
        grid=(bsz, nt),
        in_specs=[
            pl.BlockSpec((tm, d), lambda bi, t: (bi * nt + t, 0)),
            _resident(w_qkv.shape),
            _resident(w_o.shape),
            rope_spec, rope_spec, rope_spec,
            _resident(mask.shape),
            _resident(sink_col.shape),
            _resident((1, d)),
            _resident((1, d)),
        ],
        out_specs=(pl.BlockSpec((tm, d), lambda bi, t: (bi * nt + t, 0)),
                   pl.BlockSpec((1, keep, dkv), lambda bi, t: (bi, 0, 0)),
                   pl.BlockSpec((1, keep, dkv), lambda bi, t: (bi, 0, 0))),
        scratch_shapes=[
            pltpu.VMEM((2 * n_qblk, tm, LANES), BF16),
            pltpu.VMEM((n_kvblk, seq, LANES), BF16),
            pltpu.VMEM((n_kvblk, seq, LANES), BF16),
            pltpu.VMEM((n_qblk, tm, LANES), BF16),
        ],
        compiler_params=_params(2),
        name="mix_b_prompt",
    )(x, w_qkv, w_o, cos, sup, sdn, mask, sink_col, g, b)


def _mix_b_step_kernel(x_ref, wqkv_ref, wo_ref, cos_ref, sup_ref, sdn_ref, sink_ref,
                       kc_ref, vc_ref, g_ref, b_ref, o_ref, kn_ref, vn_ref,
                       q_s, att_s, *, t_new, n_qblk, n_kvblk):
    bi = pl.program_id(0)
    dq = n_qblk * LANES
    dkv = n_kvblk * LANES
    lo = _lo_lanes()

    @pl.when(bi == 0)
    def _():
        cos, sup, sdn = cos_ref[...], sup_ref[...], sdn_ref[...]
        qkv = jnp.dot(x_ref[...].astype(BF16), wqkv_ref[...], preferred_element_type=F32)
        for jq in range(n_qblk):
            blk = _rope_block(qkv[:, jq * LANES:(jq + 1) * LANES], cos, sup, sdn) * QK_SCALE
            h0, h1 = _stack_q_for_kv(blk, jq, lo)
            q_s[2 * jq] = h0.astype(BF16)
            q_s[2 * jq + 1] = h1.astype(BF16)
        for kb in range(n_kvblk):
            kn_ref[:, kb * LANES:(kb + 1) * LANES] = _rope_block(
                qkv[:, dq + kb * LANES:dq + (kb + 1) * LANES], cos, sup, sdn)
        vn_ref[...] = qkv[:, dq + dkv:]

    rows = pl.ds(pl.multiple_of(bi * t_new, t_new), t_new)
    heads_per_kvblk = 2 * n_qblk // n_kvblk
    for kb in range(n_kvblk):
        cols = slice(kb * LANES, (kb + 1) * LANES)
        q = jnp.concatenate([q_s[kb * heads_per_kvblk + i, rows, :]
                             for i in range(heads_per_kvblk)], axis=0)
        k_old = kc_ref[0, :, cols].astype(BF16)
        v_old = vc_ref[0, :, cols].astype(BF16)
        k_new = kn_ref[rows, cols].astype(BF16)
        v_new = vn_ref[rows, cols].astype(BF16)
        s_old = lax.dot_general(q, k_old, _NT, preferred_element_type=F32)
        s_new = lax.dot_general(q, k_new, _NT, preferred_element_type=F32)
        pv, denom = _softmax_pv([s_old, s_new], [v_old, v_new], sink=sink_ref[kb])
        o = pv / denom
        for a in range(heads_per_kvblk // 2):
            jq = kb * (heads_per_kvblk // 2) + a
            att_s[rows, jq * LANES:(jq + 1) * LANES] = _unstack_o(o, a, t_new, lo).astype(BF16)

    @pl.when(bi == pl.num_programs(0) - 1)
    def _():
        mix = jnp.dot(att_s[...], wo_ref[...], preferred_element_type=F32)
        o_ref[...] = _layer_norm_rows(DEEPNORM_ALPHA * x_ref[...] + mix, g_ref[...], b_ref[...])


def _mix_b_step(x, w_qkv, w_o, rope, sink_col, k_cache, v_cache, g, b, *, bsz, t_new):
    n, d = x.shape
    dq = w_o.shape[0]
    dkv = (w_qkv.shape[1] - dq) // 2
    n_qblk, n_kvblk = dq // LANES, dkv // LANES
    lc = k_cache.shape[1]
    assert t_new % 16 == 0 and lc % 16 == 0
    cos, sup, sdn = rope
    full = lambda bi: (0, 0)
    return pl.pallas_call(
        functools.partial(_mix_b_step_kernel, t_new=t_new, n_qblk=n_qblk, n_kvblk=n_kvblk),
        out_shape=(jax.ShapeDtypeStruct((n, d), F32),
                   jax.ShapeDtypeStruct((n, dkv), F32),
                   jax.ShapeDtypeStruct((n, dkv), F32)),
        grid=(bsz,),
        in_specs=[
            _resident((n, d)),
            _resident(w_qkv.shape),
            _resident(w_o.shape),
            _resident(cos.shape), _resident(sup.shape), _resident(sdn.shape),
            _resident(sink_col.shape),
            pl.BlockSpec((1, lc, dkv), lambda bi: (bi, 0, 0)),
            pl.BlockSpec((1, lc, dkv), lambda bi: (bi, 0, 0)),
            _resident((1, d)),
            _resident((1, d)),
        ],
        out_specs=(pl.BlockSpec((n, d), full),
                   pl.BlockSpec((n, dkv), full),
                   pl.BlockSpec((n, dkv), full)),
        scratch_shapes=[
            pltpu.VMEM((2 * n_qblk, n, LANES), BF16),
            pltpu.VMEM((n, dq), BF16),
        ],
        compiler_params=_params(1),
        name="mix_b_step",
    )(x, w_qkv, w_o, cos, sup, sdn, sink_col, k_cache, v_cache, g, b)


def _pair_bias_a(table):
    n_heads = table.shape[0]
    band = (A_LEFT_CHUNKS + 1) * CHUNK
    q_off = jnp.arange(CHUNK)
    k_off = jnp.arange(band) - A_LEFT_CHUNKS * CHUNK
    rel = jnp.clip(q_off[:, None] - k_off[None, :], -A_REL_CLIP, A_REL_CLIP) + A_REL_CLIP
    bias = table.astype(F32)[:, rel]
    ninf = jnp.full((n_heads, CHUNK, CHUNK), -jnp.inf, F32)
    first = jnp.concatenate([bias, ninf], axis=2)
    second = jnp.concatenate([ninf, bias], axis=2)
    pair = jnp.concatenate([first, second], axis=1)
    return pair.reshape(n_heads // 2, 2 * PAIR, band + CHUNK)


def _rope_tables(pos):
    half = HEAD_DIM // 2
    inv = ROPE_THETA ** (-jnp.arange(half, dtype=F32) / half)
    ang = pos.astype(F32)[:, None] * inv[None, :]
    cos, sin = jnp.cos(ang), jnp.sin(ang)
    zero = jnp.zeros_like(sin)
    tile = lambda a, c: jnp.concatenate([a, c] * (LANES // HEAD_DIM), axis=1)
    return tile(cos, cos), tile(zero, sin), tile(-sin, zero)


def _pair_mask_b(heads):
    win = (B_LEFT_CHUNKS + 2) * CHUNK
    r = jnp.arange(PAIR)[:, None]
    c = jnp.arange(win)[None, :]
    hidden = ((r < CHUNK) & (c >= win - CHUNK)) | ((r >= CHUNK) & (c < CHUNK))
    return jnp.tile(jnp.where(hidden, -jnp.inf, 0.0).astype(F32), (heads, 1))


def _sink_column(sinks, n_kvblk, rows):
    per = sinks.shape[0] // n_kvblk
    return jnp.repeat(sinks.astype(F32).reshape(n_kvblk, per), rows, axis=1)[..., None]


def kernel(x_prompt, x_sample, cache_a_k, cache_a_v, cache_b_k, cache_b_v, ln_g, ln_b,
           w_ffn_in, w_ffn_down, w_qkv_a, w_o_a, rel_bias_a, w_qkv_b, w_o_b, sinks_b):
    bsz, seq, d = x_prompt.shape
    sbsz, t_new, _ = x_sample.shape
    tm = 512
    xp = x_prompt.reshape(bsz * seq, d)
    xs = x_sample.reshape(sbsz * t_new, d)
    w_in = w_ffn_in.astype(BF16)
    w_down = w_ffn_down.astype(BF16)
    ln = lambda i, j: (ln_g[i, j][None], ln_b[i, j][None])

    def ffn(x, i, j, rows):
        return _ffn_ln(x, w_in[i, j], w_down[i, j], *ln(i, 2 * j), tm=rows)

    xp = ffn(xp, 0, 0, tm)
    xs = ffn(xs, 0, 0, xs.shape[0])
    wqkv_a, wo_a = w_qkv_a[0].astype(BF16), w_o_a[0].astype(BF16)
    bias2 = _pair_bias_a(rel_bias_a[0])
    d_a = wo_a.shape[0]
    xp, ak_p, av_p = _mix_a_prompt(xp, wqkv_a, wo_a, bias2, *ln(0, 1), bsz=bsz, seq=seq, tm=tm)
    xs, ak_s, av_s = _mix_a_step(xs, wqkv_a, wo_a, bias2,
                                 cache_a_k[0].reshape(sbsz, -1, d_a), cache_a_v[0].reshape(sbsz, -1, d_a),
                                 *ln(0, 1), bsz=sbsz, t_new=t_new)
    xp = ffn(xp, 0, 1, tm)
    xs = ffn(xs, 0, 1, xs.shape[0])

    xp = ffn(xp, 1, 0, tm)
    xs = ffn(xs, 1, 0, xs.shape[0])
    wqkv_b, wo_b = w_qkv_b[0].astype(BF16), w_o_b[0].astype(BF16)
    dkv = (wqkv_b.shape[1] - wo_b.shape[0]) // 2
    n_kvblk = dkv // LANES
    heads_per_kvblk = sinks_b.shape[1] // n_kvblk
    keep_b = min((B_LEFT_CHUNKS) * CHUNK, seq)
    xp, bk_p, bv_p = _mix_b_prompt(
        xp, wqkv_b, wo_b, _rope_tables(jnp.arange(seq)), _pair_mask_b(heads_per_kvblk),
        _sink_column(sinks_b[0], n_kvblk, PAIR), *ln(1, 1), bsz=bsz, seq=seq, tm=tm, keep=keep_b)
    step_pos = jnp.tile(PAST_LEN + jnp.arange(t_new), sbsz)
    xs, bk_s, bv_s = _mix_b_step(
        xs, wqkv_b, wo_b, _rope_tables(step_pos), _sink_column(sinks_b[0], n_kvblk, t_new),
        cache_b_k[0].reshape(sbsz, -1, dkv), cache_b_v[0].reshape(sbsz, -1, dkv),
        *ln(1, 1), bsz=sbsz, t_new=t_new)
    xp = ffn(xp, 1, 1, tm)
    xs = ffn(xs, 1, 1, xs.shape[0])

    heads_a = d_a // HEAD_DIM
    heads_kv = dkv // HEAD_DIM
    return (xp.reshape(bsz, seq, d), xs.reshape(sbsz, t_new, d),
            ak_p.reshape(1, bsz, -1, heads_a, HEAD_DIM), av_p.reshape(1, bsz, -1, heads_a, HEAD_DIM),
            bk_p.reshape(1, bsz, -1, heads_kv, HEAD_DIM), bv_p.reshape(1, bsz, -1, heads_kv, HEAD_DIM),
            ak_s.reshape(1, sbsz, t_new, heads_a, HEAD_DIM), av_s.reshape(1, sbsz, t_new, heads_a, HEAD_DIM),
            bk_s.reshape(1, sbsz, t_new, heads_kv, HEAD_DIM), bv_s.reshape(1, sbsz, t_new, heads_kv, HEAD_DIM))
```

```python
import functools

import jax
import jax.numpy as jnp
from jax import lax
from jax.experimental import pallas as pl
from jax.experimental.pallas import tpu as pltpu

F32 = jnp.float32
BF16 = jnp.bfloat16

DEPTH = 2
CHUNK = 64
HEAD_DIM = 64
A_LEFT_CHUNKS = 8
A_REL_CLIP = 128
B_LEFT_CHUNKS = 2
PAST_LEN = 2048
ROPE_THETA = 10000.0
LN_EPS = 1e-5
DEEPNORM_ALPHA = (2.0 * DEPTH) ** 0.25
QK_SCALE = HEAD_DIM ** -0.5

LANES = 128
PAIR = 2 * CHUNK
MXU_N = 256
V7X_VMEM_LIMIT_BYTES = 56 * 1024 * 1024

_NT = (((1,), (1,)), ((), ()))


def _params(n_axes):
    return pltpu.CompilerParams(
        dimension_semantics=("arbitrary",) * n_axes,
        vmem_limit_bytes=V7X_VMEM_LIMIT_BYTES)


def _resident(shape):
    nd = len(shape)
    return pl.BlockSpec(shape, lambda *_: (0,) * nd, pipeline_mode=pl.Buffered(1))


def _layer_norm_rows(y, g, b):
    mu = jnp.mean(y, axis=-1, keepdims=True)
    d = y - mu
    var = jnp.mean(d * d, axis=-1, keepdims=True)
    return d * lax.rsqrt(var + LN_EPS) * g + b


def _lo_lanes():
    return lax.broadcasted_iota(jnp.int32, (1, LANES), 1) < HEAD_DIM


def _ffn_ln_kernel(x_ref, win_ref, wdown_ref, g_ref, b_ref, o_ref, *, d_ff):
    x = x_ref[...]
    xb = x.astype(BF16)
    gate = jnp.dot(xb, win_ref[:, :d_ff], preferred_element_type=F32)
    up = jnp.dot(xb, win_ref[:, d_ff:], preferred_element_type=F32)
    act = (gate * jax.nn.sigmoid(gate)) * up
    y = jnp.dot(act.astype(BF16), wdown_ref[...], preferred_element_type=F32)
    o_ref[...] = _layer_norm_rows(DEEPNORM_ALPHA * x + 0.5 * y, g_ref[...], b_ref[...])


def _ffn_ln(x, w_in, w_down, g, b, *, tm):
    n, d = x.shape
    d_ff = w_down.shape[0]
    assert n % tm == 0
    return pl.pallas_call(
        functools.partial(_ffn_ln_kernel, d_ff=d_ff),
        out_shape=jax.ShapeDtypeStruct((n, d), F32),
        grid=(n // tm,),
        in_specs=[
            pl.BlockSpec((tm, d), lambda i: (i, 0)),
            _resident((d, 2 * d_ff)),
            _resident((d_ff, d)),
            _resident((1, d)),
            _resident((1, d)),
        ],
        out_specs=pl.BlockSpec((tm, d), lambda i: (i, 0)),
        compiler_params=_params(1),
        name="ffn_ln",
    )(x, w_in, w_down, g, b)


def _softmax_pv(s_parts, v_parts, sink=None):
    m = jnp.max(s_parts[0], axis=1, keepdims=True)
    for s in s_parts[1:]:
        m = jnp.maximum(m, jnp.max(s, axis=1, keepdims=True))
    if sink is not None:
        m = jnp.maximum(m, sink)
    denom = None
    pv = None
    for s, v in zip(s_parts, v_parts):
        e = jnp.exp(s - m)
        part_l = jnp.sum(e, axis=1, keepdims=True)
        part_pv = jnp.dot(e.astype(BF16), v, preferred_element_type=F32)
        denom = part_l if denom is None else denom + part_l
        pv = part_pv if pv is None else pv + part_pv
    if sink is not None:
        denom = denom + jnp.exp(sink - m)
    return pv, denom


def _split_heads(q2, lo):
    zero = jnp.zeros_like(q2)
    return jnp.concatenate([jnp.where(lo, q2, zero), jnp.where(lo, zero, q2)], axis=0)


def _attend_keys_on_sublanes(kw, qst, add_t, vwt, sink=None):
    s = lax.dot_general(kw, qst, _NT, preferred_element_type=F32) + add_t
    m = jnp.max(s, axis=0, keepdims=True)
    if sink is not None:
        m = jnp.maximum(m, sink)
    e = jnp.exp(s - m)
    denom = jnp.sum(e, axis=0, keepdims=True)
    if sink is not None:
        denom = denom + jnp.exp(sink - m)
    o_t = jnp.dot(vwt, e.astype(BF16), preferred_element_type=F32)
    return o_t / denom


def _store_vt(vt_s, blk_idx, tile_blk0, v_blk):
    vt = v_blk.T
    for c in range(v_blk.shape[0] // LANES):
        vt_s[blk_idx, tile_blk0 + c] = vt[:, c * LANES:(c + 1) * LANES].astype(BF16)


def _vt_window(vt_s, blk_idx, kblk, nkeys):
    return jnp.concatenate([vt_s[blk_idx, kblk + i] for i in range(nkeys // LANES)], axis=1)


def _mix_a_prompt_kernel(x_ref, wqkv_ref, wo_ref, bias_ref, g_ref, b_ref,
                         o_ref, kc_ref, vc_ref,
                         q_s, k_s, vt_s, att_s, *, tm, n_blk):
    t = pl.program_id(1)
    x = x_ref[...]
    xb = x.astype(BF16)
    row0 = pl.multiple_of(t * tm, tm)
    tile_blk0 = t * (tm // LANES)
    d_attn = n_blk * LANES
    for j in range(3 * d_attn // MXU_N):
        r = jnp.dot(xb, wqkv_ref[:, j * MXU_N:(j + 1) * MXU_N], preferred_element_type=F32)
        sec, jj = divmod(j, d_attn // MXU_N)
        for u in range(MXU_N // LANES):
            blk = r[:, u * LANES:(u + 1) * LANES]
            hp = jj * (MXU_N // LANES) + u
            if sec == 0:
                q_s[hp] = (blk * QK_SCALE).astype(BF16)
            elif sec == 1:
                k_s[hp, pl.ds(row0, tm), :] = blk.astype(BF16)
                kc_ref[0, :, hp * LANES:(hp + 1) * LANES] = blk
            else:
                _store_vt(vt_s, hp, tile_blk0, blk)
                vc_ref[0, :, hp * LANES:(hp + 1) * LANES] = blk

    lo = _lo_lanes()
    win = (A_LEFT_CHUNKS + 2) * CHUNK
    hp_per_iter = 2

    def scores(hp, cp, kblk, nkeys):
        q2 = q_s[hp, cp * PAIR:(cp + 1) * PAIR, :]
        kw = k_s[hp, pl.ds(pl.multiple_of(kblk * LANES, LANES), nkeys), :]
        s = lax.dot_general(kw, _split_heads(q2, lo), _NT, preferred_element_type=F32)
        return s + bias_ref[hp, win - nkeys:, :]

    def finish(hp, cp, kblk, nkeys, s):
        m = jnp.max(s, axis=0, keepdims=True)
        e = jnp.exp(s - m)
        denom = jnp.sum(e, axis=0, keepdims=True)
        o_t = jnp.dot(_vt_window(vt_s, hp, kblk, nkeys), e.astype(BF16),
                      preferred_element_type=F32) / denom
        both = jnp.concatenate([o_t[:HEAD_DIM, :PAIR], o_t[HEAD_DIM:, PAIR:]], axis=0)
        att_s[hp, cp * PAIR:(cp + 1) * PAIR, :] = both.T.astype(BF16)

    def pipelined(units):
        s = scores(*units[0])
        for i, u in enumerate(units):
            s_next = scores(*units[i + 1]) if i + 1 < len(units) else None
            finish(*u, s)
            s = s_next

    n_cp = tm // PAIR

    @pl.when(t == 0)
    def _():
        def body(hp, c):
            pipelined([(hp, cp, 0, min(win, (cp + 1) * PAIR)) for cp in range(n_cp)])
            return c
        lax.fori_loop(0, n_blk, body, 0)

    @pl.when(t > 0)
    def _():
        def body(i, c):
            pipelined([(i * hp_per_iter + dh, cp, tile_blk0 + cp + 1 - win // LANES, win)
                       for dh in range(hp_per_iter) for cp in range(n_cp)])
            return c
        lax.fori_loop(0, n_blk // hp_per_iter, body, 0)

    att = jnp.concatenate([att_s[hp] for hp in range(n_blk)], axis=1)
    mix = jnp.dot(att, wo_ref[...], preferred_element_type=F32)
    o_ref[...] = _layer_norm_rows(DEEPNORM_ALPHA * x + mix, g_ref[...], b_ref[...])


def _mix_a_prompt(x, w_qkv, w_o, bias2, g, b, *, bsz, seq, tm):
    n, d = x.shape
    d_attn = w_o.shape[0]
    n_blk = d_attn // LANES
    keep = min(A_LEFT_CHUNKS * CHUNK, seq)
    assert seq % tm == 0 and keep == tm and tm >= A_LEFT_CHUNKS * CHUNK and tm % PAIR == 0
    nt = seq // tm
    return pl.pallas_call(
        functools.partial(_mix_a_prompt_kernel, tm=tm, n_blk=n_blk),
        out_shape=(jax.ShapeDtypeStruct((n, d), F32),
                   jax.ShapeDtypeStruct((bsz, keep, d_attn), F32),
                   jax.ShapeDtypeStruct((bsz, keep, d_attn), F32)),
        grid=(bsz, nt),
        in_specs=[
            pl.BlockSpec((tm, d), lambda bi, t: (bi * nt + t, 0)),
            _resident(w_qkv.shape),
            _resident(w_o.shape),
            _resident(bias2.shape),
            _resident((1, d)),
            _resident((1, d)),
        ],
        out_specs=(pl.BlockSpec((tm, d), lambda bi, t: (bi * nt + t, 0)),
                   pl.BlockSpec((1, keep, d_attn), lambda bi, t: (bi, 0, 0)),
                   pl.BlockSpec((1, keep, d_attn), lambda bi, t: (bi, 0, 0))),
        scratch_shapes=[
            pltpu.VMEM((n_blk, tm, LANES), BF16),
            pltpu.VMEM((n_blk, seq, LANES), BF16),
            pltpu.VMEM((n_blk, seq // LANES, LANES, LANES), BF16),
            pltpu.VMEM((n_blk, tm, LANES), BF16),
        ],
        compiler_params=_params(2),
        name="mix_a_prompt",
    )(x, w_qkv, w_o, bias2, g, b)


def _mix_a_step_kernel(x_ref, wqkv_ref, wo_ref, bias_ref, kc_ref, vc_ref, g_ref, b_ref,
                       o_ref, kn_ref, vn_ref, qkv_s, att_s, *, t_new, n_blk):
    bi = pl.program_id(0)
    d_attn = n_blk * LANES
    lc = kc_ref.shape[1]

    @pl.when(bi == 0)
    def _():
        qkv = jnp.dot(x_ref[...].astype(BF16), wqkv_ref[...], preferred_element_type=F32)
        qkv_s[...] = qkv
        kn_ref[...] = qkv[:, d_attn:2 * d_attn]
        vn_ref[...] = qkv[:, 2 * d_attn:]

    lo = _lo_lanes()
    rows = pl.ds(pl.multiple_of(bi * t_new, t_new), t_new)
    for hp in range(n_blk):
        cols = slice(hp * LANES, (hp + 1) * LANES)
        q2 = (qkv_s[rows, cols] * QK_SCALE).astype(BF16)
        k_old = kc_ref[0, :, cols].astype(BF16)
        v_old = vc_ref[0, :, cols].astype(BF16)
        k_new = qkv_s[rows, d_attn + hp * LANES:d_attn + (hp + 1) * LANES].astype(BF16)
        v_new = qkv_s[rows, 2 * d_attn + hp * LANES:2 * d_attn + (hp + 1) * LANES].astype(BF16)
        qq = _split_heads(q2, lo)
        bias = bias_ref[hp]
        s_old = lax.dot_general(qq, k_old, _NT, preferred_element_type=F32) + bias[:, :lc]
        s_new = lax.dot_general(qq, k_new, _NT, preferred_element_type=F32) + bias[:, lc:lc + t_new]
        pv, denom = _softmax_pv([s_old, s_new], [v_old, v_new])
        o = pv / denom
        att_s[rows, cols] = jnp.where(lo, o[:t_new], o[t_new:]).astype(BF16)

    @pl.when(bi == pl.num_programs(0) - 1)
    def _():
        mix = jnp.dot(att_s[...], wo_ref[...], preferred_element_type=F32)
        o_ref[...] = _layer_norm_rows(DEEPNORM_ALPHA * x_ref[...] + mix, g_ref[...], b_ref[...])


def _mix_a_step(x, w_qkv, w_o, bias2, k_cache, v_cache, g, b, *, bsz, t_new):
    n, d = x.shape
    d_attn = w_o.shape[0]
    n_blk = d_attn // LANES
    lc = k_cache.shape[1]
    assert lc == A_LEFT_CHUNKS * CHUNK and t_new <= CHUNK and t_new % 16 == 0
    full = lambda bi: (0, 0)
    return pl.pallas_call(
        functools.partial(_mix_a_step_kernel, t_new=t_new, n_blk=n_blk),
        out_shape=(jax.ShapeDtypeStruct((n, d), F32),
                   jax.ShapeDtypeStruct((n, d_attn), F32),
                   jax.ShapeDtypeStruct((n, d_attn), F32)),
        grid=(bsz,),
        in_specs=[
            _resident((n, d)),
            _resident(w_qkv.shape),
            _resident(w_o.shape),
            _resident(bias2.shape),
            pl.BlockSpec((1, lc, d_attn), lambda bi: (bi, 0, 0)),
            pl.BlockSpec((1, lc, d_attn), lambda bi: (bi, 0, 0)),
            _resident((1, d)),
            _resident((1, d)),
        ],
        out_specs=(pl.BlockSpec((n, d), full),
                   pl.BlockSpec((n, d_attn), full),
                   pl.BlockSpec((n, d_attn), full)),
        scratch_shapes=[
            pltpu.VMEM((n, 3 * d_attn), F32),
            pltpu.VMEM((n, d_attn), BF16),
        ],
        compiler_params=_params(1),
        name="mix_a_step",
    )(x, w_qkv, w_o, bias2, k_cache, v_cache, g, b)


def _rope_block(blk, cos, sin_up, sin_dn):
    half = HEAD_DIM // 2
    return blk * cos + pltpu.roll(blk, half, 1) * sin_up + pltpu.roll(blk, LANES - half, 1) * sin_dn


def _stack_q_for_kv(blk, j, lo):
    zero = jnp.zeros_like(blk)
    rolled = pltpu.roll(blk, HEAD_DIM, 1)
    if (j // 2) % 2 == 0:
        return jnp.where(lo, blk, zero), jnp.where(lo, rolled, zero)
    return jnp.where(lo, zero, rolled), jnp.where(lo, zero, blk)


def _unstack_o(o, i_pair, rows, lo):
    i0, i1 = 2 * i_pair, 2 * i_pair + 1
    o0 = o[i0 * rows:(i0 + 1) * rows]
    o1 = o[i1 * rows:(i1 + 1) * rows]
    if i0 // 4 == 0:
        o1 = pltpu.roll(o1, HEAD_DIM, 1)
    else:
        o0 = pltpu.roll(o0, HEAD_DIM, 1)
    return jnp.where(lo, o0, o1)


def _mix_b_prompt_kernel(x_ref, wqkv_ref, wo_ref, cos_ref, sup_ref, sdn_ref, mask_ref, sink_ref,
                         g_ref, b_ref, o_ref, kc_ref, vc_ref,
                         q_s, k_s, vt_s, att_s, *, tm, n_qblk, n_kvblk, keep):
    t = pl.program_id(1)
    x = x_ref[...]
    xb = x.astype(BF16)
    row0 = pl.multiple_of(t * tm, tm)
    tile_blk0 = t * (tm // LANES)
    lo = _lo_lanes()
    cos, sup, sdn = cos_ref[...], sup_ref[...], sdn_ref[...]
    dq = n_qblk * LANES
    dkv = n_kvblk * LANES
    for j in range((dq + 2 * dkv) // MXU_N):
        r = jnp.dot(xb, wqkv_ref[:, j * MXU_N:(j + 1) * MXU_N], preferred_element_type=F32)
        for u in range(MXU_N // LANES):
            blk = r[:, u * LANES:(u + 1) * LANES]
            col = j * MXU_N + u * LANES
            if col < dq:
                jq = col // LANES
                h0, h1 = _stack_q_for_kv(_rope_block(blk, cos, sup, sdn) * QK_SCALE, jq, lo)
                q_s[2 * jq] = h0.astype(BF16)
                q_s[2 * jq + 1] = h1.astype(BF16)
            elif col < dq + dkv:
                kb = (col - dq) // LANES
                kr = _rope_block(blk, cos, sup, sdn)
                k_s[kb, pl.ds(row0, tm), :] = kr.astype(BF16)
                kc_ref[0, :, kb * LANES:(kb + 1) * LANES] = kr[tm - keep:]
            else:
                kb = (col - dq - dkv) // LANES
                _store_vt(vt_s, kb, tile_blk0, blk)
                vc_ref[0, :, kb * LANES:(kb + 1) * LANES] = blk[tm - keep:]

    win = (B_LEFT_CHUNKS + 2) * CHUNK
    heads_per_kvblk = 2 * n_qblk // n_kvblk

    first_tile = (t == 0).astype(jnp.int32)

    def window(cp):
        kblk = tile_blk0 + cp + 1 - win // LANES
        if cp == 0:
            return jnp.maximum(kblk, 0), mask_ref[first_tile]
        return kblk, mask_ref[0]

    def scores(kb, cp):
        q = q_s[kb * heads_per_kvblk:(kb + 1) * heads_per_kvblk, cp * PAIR:(cp + 1) * PAIR, :]
        q = q.reshape(heads_per_kvblk * PAIR, LANES)
        kblk, mask = window(cp)
        kw = k_s[kb, pl.ds(pl.multiple_of(kblk * LANES, LANES), win), :]
        return lax.dot_general(kw, q, _NT, preferred_element_type=F32) + mask

    def finish(kb, cp, s):
        sink = sink_ref[kb]
        m = jnp.maximum(jnp.max(s, axis=0, keepdims=True), sink)
        e = jnp.exp(s - m)
        denom = jnp.sum(e, axis=0, keepdims=True) + jnp.exp(sink - m)
        o_t = jnp.dot(_vt_window(vt_s, kb, window(cp)[0], win), e.astype(BF16),
                      preferred_element_type=F32) / denom
        for a in range(heads_per_kvblk // 2):
            parts = []
            for i in (2 * a, 2 * a + 1):
                r0 = HEAD_DIM * (i // (heads_per_kvblk // 2))
                parts.append(o_t[r0:r0 + HEAD_DIM, i * PAIR:(i + 1) * PAIR])
            jq = kb * (heads_per_kvblk // 2) + a
            att_s[jq, cp * PAIR:(cp + 1) * PAIR, :] = jnp.concatenate(parts, axis=0).T.astype(BF16)

    units = [(kb, cp) for kb in range(n_kvblk) for cp in range(tm // PAIR)]
    s = scores(*units[0])
    for i, u in enumerate(units):
        s_next = scores(*units[i + 1]) if i + 1 < len(units) else None
        finish(*u, s)
        s = s_next

    att = jnp.concatenate([att_s[jq] for jq in range(n_qblk)], axis=1)
    mix = jnp.dot(att, wo_ref[...], preferred_element_type=F32)
    o_ref[...] = _layer_norm_rows(DEEPNORM_ALPHA * x + mix, g_ref[...], b_ref[...])


def _mix_b_prompt(x, w_qkv, w_o, rope, mask, sink_col, g, b, *, bsz, seq, tm, keep):
    n, d = x.shape
    dq = w_o.shape[0]
    dkv = (w_qkv.shape[1] - dq) // 2
    n_qblk, n_kvblk = dq // LANES, dkv // LANES
    assert seq % tm == 0 and keep <= tm and tm % PAIR == 0 and tm >= (B_LEFT_CHUNKS + 2) * CHUNK
    assert (2 * n_qblk) % n_kvblk == 0 and (2 * n_qblk // n_kvblk) % 4 == 0
    nt = seq // tm
    cos, sup, sdn = rope
    rope_spec = pl.BlockSpec((tm, LANES), lambda bi, t: (t, 0))
    return pl.pallas_call(
        functools.partial(_mix_b_prompt_kernel, tm=tm, n_qblk=n_qblk, n_kvblk=n_kvblk, keep=keep),
        out_shape=(jax.ShapeDtypeStruct((n, d), F32),
                   jax.ShapeDtypeStruct((bsz, keep, dkv), F32),
                   jax.ShapeDtypeStruct((bsz, keep, dkv), F32)),
        grid=(bsz, nt),
        in_specs=[
            pl.BlockSpec((tm, d), lambda bi, t: (bi * nt + t, 0)),
            _resident(w_qkv.shape),
            _resident(w_o.shape),
            rope_spec, rope_spec, rope_spec,
            _resident(mask.shape),
            _resident(sink_col.shape),
            _resident((1, d)),
            _resident((1, d)),
        ],
        out_specs=(pl.BlockSpec((tm, d), lambda bi, t: (bi * nt + t, 0)),
                   pl.BlockSpec((1, keep, dkv), lambda bi, t: (bi, 0, 0)),
                   pl.BlockSpec((1, keep, dkv), lambda bi, t: (bi, 0, 0))),
        scratch_shapes=[
            pltpu.VMEM((2 * n_qblk, tm, LANES), BF16),
            pltpu.VMEM((n_kvblk, seq, LANES), BF16),
            pltpu.VMEM((n_kvblk, seq // LANES, LANES, LANES), BF16),
            pltpu.VMEM((n_qblk, tm, LANES), BF16),
        ],
        compiler_params=_params(2),
        name="mix_b_prompt",
    )(x, w_qkv, w_o, cos, sup, sdn, mask, sink_col, g, b)


def _mix_b_step_kernel(x_ref, wqkv_ref, wo_ref, cos_ref, sup_ref, sdn_ref, sink_ref,
                       kc_ref, vc_ref, g_ref, b_ref, o_ref, kn_ref, vn_ref,
                       q_s, att_s, *, t_new, n_qblk, n_kvblk):
    bi = pl.program_id(0)
    dq = n_qblk * LANES
    dkv = n_kvblk * LANES
    lo = _lo_lanes()

    @pl.when(bi == 0)
    def _():
        cos, sup, sdn = cos_ref[...], sup_ref[...], sdn_ref[...]
        qkv = jnp.dot(x_ref[...].astype(BF16), wqkv_ref[...], preferred_element_type=F32)
        for jq in range(n_qblk):
            blk = _rope_block(qkv[:, jq * LANES:(jq + 1) * LANES], cos, sup, sdn) * QK_SCALE
            h0, h1 = _stack_q_for_kv(blk, jq, lo)
            q_s[2 * jq] = h0.astype(BF16)
            q_s[2 * jq + 1] = h1.astype(BF16)
        for kb in range(n_kvblk):
            kn_ref[:, kb * LANES:(kb + 1) * LANES] = _rope_block(
                qkv[:, dq + kb * LANES:dq + (kb + 1) * LANES], cos, sup, sdn)
        vn_ref[...] = qkv[:, dq + dkv:]

    rows = pl.ds(pl.multiple_of(bi * t_new, t_new), t_new)
    heads_per_kvblk = 2 * n_qblk // n_kvblk
    for kb in range(n_kvblk):
        cols = slice(kb * LANES, (kb + 1) * LANES)
        q = jnp.concatenate([q_s[kb * heads_per_kvblk + i, rows, :]
                             for i in range(heads_per_kvblk)], axis=0)
        k_old = kc_ref[0, :, cols].astype(BF16)
        v_old = vc_ref[0, :, cols].astype(BF16)
        k_new = kn_ref[rows, cols].astype(BF16)
        v_new = vn_ref[rows, cols].astype(BF16)
        s_old = lax.dot_general(q, k_old, _NT, preferred_element_type=F32)
        s_new = lax.dot_general(q, k_new, _NT, preferred_element_type=F32)
        pv, denom = _softmax_pv([s_old, s_new], [v_old, v_new], sink=sink_ref[kb])
        o = pv / denom
        for a in range(heads_per_kvblk // 2):
            jq = kb * (heads_per_kvblk // 2) + a
            att_s[rows, jq * LANES:(jq + 1) * LANES] = _unstack_o(o, a, t_new, lo).astype(BF16)

    @pl.when(bi == pl.num_programs(0) - 1)
    def _():
        mix = jnp.dot(att_s[...], wo_ref[...], preferred_element_type=F32)
        o_ref[...] = _layer_norm_rows(DEEPNORM_ALPHA * x_ref[...] + mix, g_ref[...], b_ref[...])


def _mix_b_step(x, w_qkv, w_o, rope, sink_col, k_cache, v_cache, g, b, *, bsz, t_new):
    n, d = x.shape
    dq = w_o.shape[0]
    dkv = (w_qkv.shape[1] - dq) // 2
    n_qblk, n_kvblk = dq // LANES, dkv // LANES
    lc = k_cache.shape[1]
    assert t_new % 16 == 0 and lc % 16 == 0
    cos, sup, sdn = rope
    full = lambda bi: (0, 0)
    return pl.pallas_call(
        functools.partial(_mix_b_step_kernel, t_new=t_new, n_qblk=n_qblk, n_kvblk=n_kvblk),
        out_shape=(jax.ShapeDtypeStruct((n, d), F32),
                   jax.ShapeDtypeStruct((n, dkv), F32),
                   jax.ShapeDtypeStruct((n, dkv), F32)),
        grid=(bsz,),
        in_specs=[
            _resident((n, d)),
            _resident(w_qkv.shape),
            _resident(w_o.shape),
            _resident(cos.shape), _resident(sup.shape), _resident(sdn.shape),
            _resident(sink_col.shape),
            pl.BlockSpec((1, lc, dkv), lambda bi: (bi, 0, 0)),
            pl.BlockSpec((1, lc, dkv), lambda bi: (bi, 0, 0)),
            _resident((1, d)),
            _resident((1, d)),
        ],
        out_specs=(pl.BlockSpec((n, d), full),
                   pl.BlockSpec((n, dkv), full),
                   pl.BlockSpec((n, dkv), full)),
        scratch_shapes=[
            pltpu.VMEM((2 * n_qblk, n, LANES), BF16),
            pltpu.VMEM((n, dq), BF16),
        ],
        compiler_params=_params(1),
        name="mix_b_step",
    )(x, w_qkv, w_o, cos, sup, sdn, sink_col, k_cache, v_cache, g, b)


def _pair_bias_a(table):
    n_heads = table.shape[0]
    band = (A_LEFT_CHUNKS + 1) * CHUNK
    q_off = jnp.arange(CHUNK)
    k_off = jnp.arange(band) - A_LEFT_CHUNKS * CHUNK
    rel = jnp.clip(q_off[:, None] - k_off[None, :], -A_REL_CLIP, A_REL_CLIP) + A_REL_CLIP
    bias = table.astype(F32)[:, rel]
    ninf = jnp.full((n_heads, CHUNK, CHUNK), -jnp.inf, F32)
    first = jnp.concatenate([bias, ninf], axis=2)
    second = jnp.concatenate([ninf, bias], axis=2)
    pair = jnp.concatenate([first, second], axis=1)
    return pair.reshape(n_heads // 2, 2 * PAIR, band + CHUNK)


def _rope_tables(pos):
    half = HEAD_DIM // 2
    inv = ROPE_THETA ** (-jnp.arange(half, dtype=F32) / half)
    ang = pos.astype(F32)[:, None] * inv[None, :]
    cos, sin = jnp.cos(ang), jnp.sin(ang)
    zero = jnp.zeros_like(sin)
    tile = lambda a, c: jnp.concatenate([a, c] * (LANES // HEAD_DIM), axis=1)
    return tile(cos, cos), tile(zero, sin), tile(-sin, zero)


def _pair_masks_b(heads):
    win = (B_LEFT_CHUNKS + 2) * CHUNK
    k = jnp.arange(win)[:, None]
    q = jnp.arange(PAIR)[None, :]
    hidden = ((q < CHUNK) & (k >= win - CHUNK)) | ((q >= CHUNK) & (k < CHUNK))
    hidden_first = ((q < CHUNK) & (k >= CHUNK)) | ((q >= CHUNK) & (k >= PAIR))
    masks = jnp.where(jnp.stack([hidden, hidden_first]), -jnp.inf, 0.0).astype(F32)
    return jnp.tile(masks, (1, 1, heads))


def _sink_column(sinks, n_kvblk, rows):
    per = sinks.shape[0] // n_kvblk
    return jnp.repeat(sinks.astype(F32).reshape(n_kvblk, per), rows, axis=1)[..., None]


def kernel(x_prompt, x_sample, cache_a_k, cache_a_v, cache_b_k, cache_b_v, ln_g, ln_b,
           w_ffn_in, w_ffn_down, w_qkv_a, w_o_a, rel_bias_a, w_qkv_b, w_o_b, sinks_b):
    bsz, seq, d = x_prompt.shape
    sbsz, t_new, _ = x_sample.shape
    tm = 512
    xp = x_prompt.reshape(bsz * seq, d)
    xs = x_sample.reshape(sbsz * t_new, d)
    w_in = w_ffn_in.astype(BF16)
    w_down = w_ffn_down.astype(BF16)
    ln = lambda i, j: (ln_g[i, j][None], ln_b[i, j][None])

    def ffn(x, i, j, rows):
        return _ffn_ln(x, w_in[i, j], w_down[i, j], *ln(i, 2 * j), tm=rows)

    xp = ffn(xp, 0, 0, tm)
    xs = ffn(xs, 0, 0, xs.shape[0])
    wqkv_a, wo_a = w_qkv_a[0].astype(BF16), w_o_a[0].astype(BF16)
    bias2 = _pair_bias_a(rel_bias_a[0])
    d_a = wo_a.shape[0]
    bias2_t = jnp.transpose(bias2, (0, 2, 1))
    bias_step = jnp.concatenate([bias2[:, :t_new], bias2[:, PAIR:PAIR + t_new]], axis=1)
    xp, ak_p, av_p = _mix_a_prompt(xp, wqkv_a, wo_a, bias2_t, *ln(0, 1), bsz=bsz, seq=seq, tm=tm)
    xs, ak_s, av_s = _mix_a_step(xs, wqkv_a, wo_a, bias_step,
                                 cache_a_k[0].reshape(sbsz, -1, d_a), cache_a_v[0].reshape(sbsz, -1, d_a),
                                 *ln(0, 1), bsz=sbsz, t_new=t_new)
    xp = ffn(xp, 0, 1, tm)
    xs = ffn(xs, 0, 1, xs.shape[0])

    xp = ffn(xp, 1, 0, tm)
    xs = ffn(xs, 1, 0, xs.shape[0])
    wqkv_b, wo_b = w_qkv_b[0].astype(BF16), w_o_b[0].astype(BF16)
    dkv = (wqkv_b.shape[1] - wo_b.shape[0]) // 2
    n_kvblk = dkv // LANES
    heads_per_kvblk = sinks_b.shape[1] // n_kvblk
    keep_b = min((B_LEFT_CHUNKS) * CHUNK, seq)
    xp, bk_p, bv_p = _mix_b_prompt(
        xp, wqkv_b, wo_b, _rope_tables(jnp.arange(seq)), _pair_masks_b(heads_per_kvblk),
        jnp.transpose(_sink_column(sinks_b[0], n_kvblk, PAIR), (0, 2, 1)),
        *ln(1, 1), bsz=bsz, seq=seq, tm=tm, keep=keep_b)
    step_pos = jnp.tile(PAST_LEN + jnp.arange(t_new), sbsz)
    xs, bk_s, bv_s = _mix_b_step(
        xs, wqkv_b, wo_b, _rope_tables(step_pos), _sink_column(sinks_b[0], n_kvblk, t_new),
        cache_b_k[0].reshape(sbsz, -1, dkv), cache_b_v[0].reshape(sbsz, -1, dkv),
        *ln(1, 1), bsz=sbsz, t_new=t_new)
    xp = ffn(xp, 1, 1, tm)
    xs = ffn(xs, 1, 1, xs.shape[0])

    heads_a = d_a // HEAD_DIM
    heads_kv = dkv // HEAD_DIM
    return (xp.reshape(bsz, seq, d), xs.reshape(sbsz, t_new, d),
            ak_p.reshape(1, bsz, -1, heads_a, HEAD_DIM), av_p.reshape(1, bsz, -1, heads_a, HEAD_DIM),
            bk_p.reshape(1, bsz, -1, heads_kv, HEAD_DIM), bv_p.reshape(1, bsz, -1, heads_kv, HEAD_DIM),
            ak_s.reshape(1, sbsz, t_new, heads_a, HEAD_DIM), av_s.reshape(1, sbsz, t_new, heads_a, HEAD_DIM),
            bk_s.reshape(1, sbsz, t_new, heads_kv, HEAD_DIM), bv_s.reshape(1, sbsz, t_new, heads_kv, HEAD_DIM))
```

```python
import functools

import jax
import jax.numpy as jnp
from jax import lax
from jax.experimental import pallas as pl
from jax.experimental.pallas import tpu as pltpu

F32 = jnp.float32
BF16 = jnp.bfloat16

DEPTH = 2
CHUNK = 64
HEAD_DIM = 64
A_LEFT_CHUNKS = 8
A_REL_CLIP = 128
B_LEFT_CHUNKS = 2
PAST_LEN = 2048
ROPE_THETA = 10000.0
LN_EPS = 1e-5
DEEPNORM_ALPHA = (2.0 * DEPTH) ** 0.25
QK_SCALE = HEAD_DIM ** -0.5

LANES = 128
PAIR = 2 * CHUNK
MXU_N = 256
FFN_ROW_GROUP = 256
V7X_VMEM_LIMIT_BYTES = 56 * 1024 * 1024

_NT = (((1,), (1,)), ((), ()))


def _params(n_axes):
    return pltpu.CompilerParams(
        dimension_semantics=("arbitrary",) * n_axes,
        vmem_limit_bytes=V7X_VMEM_LIMIT_BYTES)


def _resident(shape):
    nd = len(shape)
    return pl.BlockSpec(shape, lambda *_: (0,) * nd, pipeline_mode=pl.Buffered(1))


def _layer_norm_rows(y, g, b):
    mu = jnp.mean(y, axis=-1, keepdims=True)
    d = y - mu
    var = jnp.mean(d * d, axis=-1, keepdims=True)
    return d * lax.rsqrt(var + LN_EPS) * g + b


def _lo_lanes():
    return lax.broadcasted_iota(jnp.int32, (1, LANES), 1) < HEAD_DIM


def _ffn_ln_kernel(x_ref, win_ref, wdown_ref, g_ref, b_ref, o_ref, *, d_ff):
    tm = x_ref.shape[0]
    group = min(tm, FFN_ROW_GROUP)
    halves = [slice(r, r + group) for r in range(0, tm, group)]

    def hidden(rows):
        xb = x_ref[rows, :].astype(BF16)
        gate = jnp.dot(xb, win_ref[:, :d_ff], preferred_element_type=F32)
        up = jnp.dot(xb, win_ref[:, d_ff:], preferred_element_type=F32)
        return ((gate * jax.nn.sigmoid(gate)) * up).astype(BF16)

    def finish(rows, act):
        y = jnp.dot(act, wdown_ref[...], preferred_element_type=F32)
        o_ref[rows, :] = _layer_norm_rows(DEEPNORM_ALPHA * x_ref[rows, :] + 0.5 * y,
                                          g_ref[...], b_ref[...])

    act = hidden(halves[0])
    for i, rows in enumerate(halves):
        act_next = hidden(halves[i + 1]) if i + 1 < len(halves) else None
        finish(rows, act)
        act = act_next


def _ffn_ln(x, w_in, w_down, g, b, *, layer, slot, tm):
    n, d = x.shape
    d_ff = w_down.shape[2]
    assert n % tm == 0
    pick = lambda i: (layer, slot, 0, 0)
    return pl.pallas_call(
        functools.partial(_ffn_ln_kernel, d_ff=d_ff),
        out_shape=jax.ShapeDtypeStruct((n, d), F32),
        grid=(n // tm,),
        in_specs=[
            pl.BlockSpec((tm, d), lambda i: (i, 0)),
            pl.BlockSpec((None, None, d, 2 * d_ff), pick, pipeline_mode=pl.Buffered(1)),
            pl.BlockSpec((None, None, d_ff, d), pick, pipeline_mode=pl.Buffered(1)),
            _resident((1, d)),
            _resident((1, d)),
        ],
        out_specs=pl.BlockSpec((tm, d), lambda i: (i, 0)),
        compiler_params=_params(1),
        name="ffn_ln",
    )(x, w_in, w_down, g, b)


def _softmax_pv(s_parts, v_parts, sink=None):
    m = jnp.max(s_parts[0], axis=1, keepdims=True)
    for s in s_parts[1:]:
        m = jnp.maximum(m, jnp.max(s, axis=1, keepdims=True))
    if sink is not None:
        m = jnp.maximum(m, sink)
    denom = None
    pv = None
    for s, v in zip(s_parts, v_parts):
        e = jnp.exp(s - m)
        part_l = jnp.sum(e, axis=1, keepdims=True)
        part_pv = jnp.dot(e.astype(BF16), v, preferred_element_type=F32)
        denom = part_l if denom is None else denom + part_l
        pv = part_pv if pv is None else pv + part_pv
    if sink is not None:
        denom = denom + jnp.exp(sink - m)
    return pv, denom


def _split_heads(q2, lo):
    zero = jnp.zeros_like(q2)
    return jnp.concatenate([jnp.where(lo, q2, zero), jnp.where(lo, zero, q2)], axis=0)


def _attend_keys_on_sublanes(kw, qst, add_t, vwt, sink=None):
    s = lax.dot_general(kw, qst, _NT, preferred_element_type=F32) + add_t
    m = jnp.max(s, axis=0, keepdims=True)
    if sink is not None:
        m = jnp.maximum(m, sink)
    e = jnp.exp(s - m)
    denom = jnp.sum(e, axis=0, keepdims=True)
    if sink is not None:
        denom = denom + jnp.exp(sink - m)
    o_t = jnp.dot(vwt, e.astype(BF16), preferred_element_type=F32)
    return o_t / denom


def _store_vt(vt_s, blk_idx, tile_blk0, v_blk):
    vt = v_blk.T
    for c in range(v_blk.shape[0] // LANES):
        vt_s[blk_idx, tile_blk0 + c] = vt[:, c * LANES:(c + 1) * LANES].astype(BF16)


def _vt_window(vt_s, blk_idx, kblk, nkeys):
    return jnp.concatenate([vt_s[blk_idx, kblk + i] for i in range(nkeys // LANES)], axis=1)


def _mix_a_prompt_kernel(x_ref, wqkv_ref, wo_ref, bias_ref, g_ref, b_ref,
                         o_ref, kc_ref, vc_ref,
                         q_s, k_s, vt_s, att_s, *, tm, n_blk):
    t = pl.program_id(1)
    x = x_ref[...]
    xb = x.astype(BF16)
    row0 = pl.multiple_of(t * tm, tm)
    tile_blk0 = t * (tm // LANES)
    d_attn = n_blk * LANES
    for j in range(3 * d_attn // MXU_N):
        r = jnp.dot(xb, wqkv_ref[:, j * MXU_N:(j + 1) * MXU_N], preferred_element_type=F32)
        sec, jj = divmod(j, d_attn // MXU_N)
        for u in range(MXU_N // LANES):
            blk = r[:, u * LANES:(u + 1) * LANES]
            hp = jj * (MXU_N // LANES) + u
            if sec == 0:
                q_s[hp] = (blk * QK_SCALE).astype(BF16)
            elif sec == 1:
                k_s[hp, pl.ds(row0, tm), :] = blk.astype(BF16)
                kc_ref[0, :, hp * LANES:(hp + 1) * LANES] = blk
            else:
                _store_vt(vt_s, hp, tile_blk0, blk)
                vc_ref[0, :, hp * LANES:(hp + 1) * LANES] = blk

    lo = _lo_lanes()
    win = (A_LEFT_CHUNKS + 2) * CHUNK
    hp_per_iter = 4

    def scores(hp, cp0, kblk, span, subs):
        qq = jnp.concatenate([_split_heads(q_s[hp, (cp0 + i) * PAIR:(cp0 + i + 1) * PAIR, :], lo)
                              for i in range(2)], axis=0)
        kw = k_s[hp, pl.ds(pl.multiple_of(kblk * LANES, LANES), span), :]
        s = lax.dot_general(kw, qq, _NT, preferred_element_type=F32)
        return [s[off:off + nk, i * 2 * PAIR:(i + 1) * 2 * PAIR] + bias_ref[hp, win - nk:, :]
                for i, (off, nk) in enumerate(subs)]

    def finish(hp, cp0, kblk, span, subs, s_parts):
        e_cols, denoms = [], []
        for (off, nk), s in zip(subs, s_parts):
            m = jnp.max(s, axis=0, keepdims=True)
            e = jnp.exp(s - m)
            denoms.append(jnp.sum(e, axis=0, keepdims=True))
            e = e.astype(BF16)
            pads = [jnp.zeros((n, 2 * PAIR), BF16) for n in (off, span - off - nk)]
            e_cols.append(jnp.concatenate([p for p in (pads[0], e, pads[1]) if p.shape[0]], axis=0))
        o_t = jnp.dot(_vt_window(vt_s, hp, kblk, span), jnp.concatenate(e_cols, axis=1),
                      preferred_element_type=F32)
        for i, denom in enumerate(denoms):
            o_i = o_t[:, i * 2 * PAIR:(i + 1) * 2 * PAIR] / denom
            both = jnp.concatenate([o_i[:HEAD_DIM, :PAIR], o_i[HEAD_DIM:, PAIR:]], axis=0)
            att_s[hp, (cp0 + i) * PAIR:(cp0 + i + 1) * PAIR, :] = both.T.astype(BF16)

    def pipelined(units):
        s = scores(*units[0])
        for i, u in enumerate(units):
            s_next = scores(*units[i + 1]) if i + 1 < len(units) else None
            finish(*u, s)
            s = s_next

    n_cp = tm // PAIR
    assert n_cp % 2 == 0

    @pl.when(t == 0)
    def _():
        def body(hp, c):
            pipelined([(hp, cp0, 0, (cp0 + 2) * PAIR, [(0, (cp0 + 1) * PAIR), (0, (cp0 + 2) * PAIR)])
                       for cp0 in range(0, n_cp, 2)])
            return c
        lax.fori_loop(0, n_blk, body, 0)

    @pl.when(t > 0)
    def _():
        def body(i, c):
            pipelined([(i * hp_per_iter + dh, cp0, tile_blk0 + cp0 + 1 - win // LANES, win + PAIR,
                        [(0, win), (PAIR, win)])
                       for dh in range(hp_per_iter) for cp0 in range(0, n_cp, 2)])
            return c
        lax.fori_loop(0, n_blk // hp_per_iter, body, 0)

    att = jnp.concatenate([att_s[hp] for hp in range(n_blk)], axis=1)
    mix = jnp.dot(att, wo_ref[...], preferred_element_type=F32)
    o_ref[...] = _layer_norm_rows(DEEPNORM_ALPHA * x + mix, g_ref[...], b_ref[...])


def _mix_a_prompt(x, w_qkv, w_o, bias2, g, b, *, bsz, seq, tm):
    n, d = x.shape
    d_attn = w_o.shape[0]
    n_blk = d_attn // LANES
    keep = min(A_LEFT_CHUNKS * CHUNK, seq)
    assert seq % tm == 0 and keep == tm and tm >= A_LEFT_CHUNKS * CHUNK and tm % PAIR == 0
    nt = seq // tm
    return pl.pallas_call(
        functools.partial(_mix_a_prompt_kernel, tm=tm, n_blk=n_blk),
        out_shape=(jax.ShapeDtypeStruct((n, d), F32),
                   jax.ShapeDtypeStruct((bsz, keep, d_attn), F32),
                   jax.ShapeDtypeStruct((bsz, keep, d_attn), F32)),
        grid=(bsz, nt),
        in_specs=[
            pl.BlockSpec((tm, d), lambda bi, t: (bi * nt + t, 0)),
            _resident(w_qkv.shape),
            _resident(w_o.shape),
            _resident(bias2.shape),
            _resident((1, d)),
            _resident((1, d)),
        ],
        out_specs=(pl.BlockSpec((tm, d), lambda bi, t: (bi * nt + t, 0)),
                   pl.BlockSpec((1, keep, d_attn), lambda bi, t: (bi, 0, 0)),
                   pl.BlockSpec((1, keep, d_attn), lambda bi, t: (bi, 0, 0))),
        scratch_shapes=[
            pltpu.VMEM((n_blk, tm, LANES), BF16),
            pltpu.VMEM((n_blk, seq, LANES), BF16),
            pltpu.VMEM((n_blk, seq // LANES, LANES, LANES), BF16),
            pltpu.VMEM((n_blk, tm, LANES), BF16),
        ],
        compiler_params=_params(2),
        name="mix_a_prompt",
    )(x, w_qkv, w_o, bias2, g, b)


def _softmax_pv_cached(s_old, s_new, vt_old, v_new, sink=None):
    m = jnp.maximum(jnp.max(s_old, axis=1, keepdims=True), jnp.max(s_new, axis=1, keepdims=True))
    if sink is not None:
        m = jnp.maximum(m, sink)
    e_old = jnp.exp(s_old - m)
    e_new = jnp.exp(s_new - m)
    denom = jnp.sum(e_old, axis=1, keepdims=True) + jnp.sum(e_new, axis=1, keepdims=True)
    if sink is not None:
        denom = denom + jnp.exp(sink - m)
    pv = (lax.dot_general(e_old.astype(BF16), vt_old, _NT, preferred_element_type=F32)
          + jnp.dot(e_new.astype(BF16), v_new, preferred_element_type=F32))
    return pv, denom


def _mix_a_step_kernel(x_ref, wqkv_ref, wo_ref, bias_ref, kct_ref, vct_ref, g_ref, b_ref,
                       o_ref, kn_ref, vn_ref, qkv_s, att_s, *, t_new, n_blk):
    bi = pl.program_id(0)
    d_attn = n_blk * LANES
    lc = kct_ref.shape[2]

    @pl.when(bi == 0)
    def _():
        qkv = jnp.dot(x_ref[...].astype(BF16), wqkv_ref[...], preferred_element_type=F32)
        qkv_s[...] = qkv
        kn_ref[...] = qkv[:, d_attn:2 * d_attn]
        vn_ref[...] = qkv[:, 2 * d_attn:]

    lo = _lo_lanes()
    rows = pl.ds(pl.multiple_of(bi * t_new, t_new), t_new)
    for hp in range(n_blk):
        cols = slice(hp * LANES, (hp + 1) * LANES)
        q2 = (qkv_s[rows, cols] * QK_SCALE).astype(BF16)
        kt_old = kct_ref[0, cols, :].astype(BF16)
        vt_old = vct_ref[0, cols, :].astype(BF16)
        k_new = qkv_s[rows, d_attn + hp * LANES:d_attn + (hp + 1) * LANES].astype(BF16)
        v_new = qkv_s[rows, 2 * d_attn + hp * LANES:2 * d_attn + (hp + 1) * LANES].astype(BF16)
        qq = _split_heads(q2, lo)
        bias = bias_ref[hp]
        s_old = jnp.dot(qq, kt_old, preferred_element_type=F32) + bias[:, :lc]
        s_new = lax.dot_general(qq, k_new, _NT, preferred_element_type=F32) + bias[:, lc:lc + t_new]
        pv, denom = _softmax_pv_cached(s_old, s_new, vt_old, v_new)
        o = pv / denom
        att_s[rows, cols] = jnp.where(lo, o[:t_new], o[t_new:]).astype(BF16)

    @pl.when(bi == pl.num_programs(0) - 1)
    def _():
        mix = jnp.dot(att_s[...], wo_ref[...], preferred_element_type=F32)
        o_ref[...] = _layer_norm_rows(DEEPNORM_ALPHA * x_ref[...] + mix, g_ref[...], b_ref[...])


def _mix_a_step(x, w_qkv, w_o, bias2, k_cache, v_cache, g, b, *, bsz, t_new):
    n, d = x.shape
    d_attn = w_o.shape[0]
    n_blk = d_attn // LANES
    lc = k_cache.shape[2]
    assert lc == A_LEFT_CHUNKS * CHUNK and t_new <= CHUNK and t_new % 16 == 0
    full = lambda bi: (0, 0)
    return pl.pallas_call(
        functools.partial(_mix_a_step_kernel, t_new=t_new, n_blk=n_blk),
        out_shape=(jax.ShapeDtypeStruct((n, d), F32),
                   jax.ShapeDtypeStruct((n, d_attn), F32),
                   jax.ShapeDtypeStruct((n, d_attn), F32)),
        grid=(bsz,),
        in_specs=[
            _resident((n, d)),
            _resident(w_qkv.shape),
            _resident(w_o.shape),
            _resident(bias2.shape),
            pl.BlockSpec((1, d_attn, lc), lambda bi: (bi, 0, 0)),
            pl.BlockSpec((1, d_attn, lc), lambda bi: (bi, 0, 0)),
            _resident((1, d)),
            _resident((1, d)),
        ],
        out_specs=(pl.BlockSpec((n, d), full),
                   pl.BlockSpec((n, d_attn), full),
                   pl.BlockSpec((n, d_attn), full)),
        scratch_shapes=[
            pltpu.VMEM((n, 3 * d_attn), F32),
            pltpu.VMEM((n, d_attn), BF16),
        ],
        compiler_params=_params(1),
        name="mix_a_step",
    )(x, w_qkv, w_o, bias2, k_cache, v_cache, g, b)


def _rope_block(blk, cos, sin_up, sin_dn):
    half = HEAD_DIM // 2
    return blk * cos + pltpu.roll(blk, half, 1) * sin_up + pltpu.roll(blk, LANES - half, 1) * sin_dn


def _stack_q_for_kv(blk, j, lo):
    zero = jnp.zeros_like(blk)
    rolled = pltpu.roll(blk, HEAD_DIM, 1)
    if (j // 2) % 2 == 0:
        return jnp.where(lo, blk, zero), jnp.where(lo, rolled, zero)
    return jnp.where(lo, zero, rolled), jnp.where(lo, zero, blk)


def _unstack_o(o, i_pair, rows, lo):
    i0, i1 = 2 * i_pair, 2 * i_pair + 1
    o0 = o[i0 * rows:(i0 + 1) * rows]
    o1 = o[i1 * rows:(i1 + 1) * rows]
    if i0 // 4 == 0:
        o1 = pltpu.roll(o1, HEAD_DIM, 1)
    else:
        o0 = pltpu.roll(o0, HEAD_DIM, 1)
    return jnp.where(lo, o0, o1)


def _mix_b_prompt_kernel(x_ref, wqkv_ref, wo_ref, cos_ref, sup_ref, sdn_ref, mask_ref, sink_ref,
                         g_ref, b_ref, o_ref, kc_ref, vc_ref,
                         q_s, k_s, vt_s, att_s, *, tm, n_qblk, n_kvblk, keep):
    t = pl.program_id(1)
    x = x_ref[...]
    xb = x.astype(BF16)
    row0 = pl.multiple_of(t * tm, tm)
    tile_blk0 = t * (tm // LANES)
    lo = _lo_lanes()
    cos, sup, sdn = cos_ref[...], sup_ref[...], sdn_ref[...]
    dq = n_qblk * LANES
    dkv = n_kvblk * LANES
    for j in range((dq + 2 * dkv) // MXU_N):
        r = jnp.dot(xb, wqkv_ref[:, j * MXU_N:(j + 1) * MXU_N], preferred_element_type=F32)
        for u in range(MXU_N // LANES):
            blk = r[:, u * LANES:(u + 1) * LANES]
            col = j * MXU_N + u * LANES
            if col < dq:
                jq = col // LANES
                h0, h1 = _stack_q_for_kv(_rope_block(blk, cos, sup, sdn) * QK_SCALE, jq, lo)
                q_s[2 * jq] = h0.astype(BF16)
                q_s[2 * jq + 1] = h1.astype(BF16)
            elif col < dq + dkv:
                kb = (col - dq) // LANES
                kr = _rope_block(blk, cos, sup, sdn)
                k_s[kb, pl.ds(row0, tm), :] = kr.astype(BF16)
                kc_ref[0, :, kb * LANES:(kb + 1) * LANES] = kr[tm - keep:]
            else:
                kb = (col - dq - dkv) // LANES
                _store_vt(vt_s, kb, tile_blk0, blk)
                vc_ref[0, :, kb * LANES:(kb + 1) * LANES] = blk[tm - keep:]

    win = (B_LEFT_CHUNKS + 2) * CHUNK
    heads_per_kvblk = 2 * n_qblk // n_kvblk

    first_tile = (t == 0).astype(jnp.int32)

    def window(cp):
        kblk = tile_blk0 + cp + 1 - win // LANES
        if cp == 0:
            return jnp.maximum(kblk, 0), mask_ref[first_tile]
        return kblk, mask_ref[0]

    def scores(kb, cp):
        q = q_s[kb * heads_per_kvblk:(kb + 1) * heads_per_kvblk, cp * PAIR:(cp + 1) * PAIR, :]
        q = q.reshape(heads_per_kvblk * PAIR, LANES)
        kblk, mask = window(cp)
        kw = k_s[kb, pl.ds(pl.multiple_of(kblk * LANES, LANES), win), :]
        return lax.dot_general(kw, q, _NT, preferred_element_type=F32) + mask

    def finish(kb, cp, s):
        sink = sink_ref[kb]
        m = jnp.maximum(jnp.max(s, axis=0, keepdims=True), sink)
        e = jnp.exp(s - m)
        denom = jnp.sum(e, axis=0, keepdims=True) + jnp.exp(sink - m)
        o_t = jnp.dot(_vt_window(vt_s, kb, window(cp)[0], win), e.astype(BF16),
                      preferred_element_type=F32) / denom
        for a in range(heads_per_kvblk // 2):
            parts = []
            for i in (2 * a, 2 * a + 1):
                r0 = HEAD_DIM * (i // (heads_per_kvblk // 2))
                parts.append(o_t[r0:r0 + HEAD_DIM, i * PAIR:(i + 1) * PAIR])
            jq = kb * (heads_per_kvblk // 2) + a
            att_s[jq, cp * PAIR:(cp + 1) * PAIR, :] = jnp.concatenate(parts, axis=0).T.astype(BF16)

    units = [(kb, cp) for kb in range(n_kvblk) for cp in range(tm // PAIR)]
    s = scores(*units[0])
    for i, u in enumerate(units):
        s_next = scores(*units[i + 1]) if i + 1 < len(units) else None
        finish(*u, s)
        s = s_next

    att = jnp.concatenate([att_s[jq] for jq in range(n_qblk)], axis=1)
    mix = jnp.dot(att, wo_ref[...], preferred_element_type=F32)
    o_ref[...] = _layer_norm_rows(DEEPNORM_ALPHA * x + mix, g_ref[...], b_ref[...])


def _mix_b_prompt(x, w_qkv, w_o, rope, mask, sink_col, g, b, *, bsz, seq, tm, keep):
    n, d = x.shape
    dq = w_o.shape[0]
    dkv = (w_qkv.shape[1] - dq) // 2
    n_qblk, n_kvblk = dq // LANES, dkv // LANES
    assert seq % tm == 0 and keep <= tm and tm % PAIR == 0 and tm >= (B_LEFT_CHUNKS + 2) * CHUNK
    assert (2 * n_qblk) % n_kvblk == 0 and (2 * n_qblk // n_kvblk) % 4 == 0
    nt = seq // tm
    cos, sup, sdn = rope
    rope_spec = pl.BlockSpec((tm, LANES), lambda bi, t: (t, 0))
    return pl.pallas_call(
        functools.partial(_mix_b_prompt_kernel, tm=tm, n_qblk=n_qblk, n_kvblk=n_kvblk, keep=keep),
        out_shape=(jax.ShapeDtypeStruct((n, d), F32),
                   jax.ShapeDtypeStruct((bsz, keep, dkv), F32),
                   jax.ShapeDtypeStruct((bsz, keep, dkv), F32)),
        grid=(bsz, nt),
        in_specs=[
            pl.BlockSpec((tm, d), lambda bi, t: (bi * nt + t, 0)),
            _resident(w_qkv.shape),
            _resident(w_o.shape),
            rope_spec, rope_spec, rope_spec,
            _resident(mask.shape),
            _resident(sink_col.shape),
            _resident((1, d)),
            _resident((1, d)),
        ],
        out_specs=(pl.BlockSpec((tm, d), lambda bi, t: (bi * nt + t, 0)),
                   pl.BlockSpec((1, keep, dkv), lambda bi, t: (bi, 0, 0)),
                   pl.BlockSpec((1, keep, dkv), lambda bi, t: (bi, 0, 0))),
        scratch_shapes=[
            pltpu.VMEM((2 * n_qblk, tm, LANES), BF16),
            pltpu.VMEM((n_kvblk, seq, LANES), BF16),
            pltpu.VMEM((n_kvblk, seq // LANES, LANES, LANES), BF16),
            pltpu.VMEM((n_qblk, tm, LANES), BF16),
        ],
        compiler_params=_params(2),
        name="mix_b_prompt",
    )(x, w_qkv, w_o, cos, sup, sdn, mask, sink_col, g, b)


def _mix_b_step_kernel(x_ref, wqkv_ref, wo_ref, cos_ref, sup_ref, sdn_ref, sink_ref,
                       kc_ref, vc_ref, g_ref, b_ref, o_ref, kn_ref, vn_ref,
                       q_s, att_s, *, t_new, n_qblk, n_kvblk):
    bi = pl.program_id(0)
    dq = n_qblk * LANES
    dkv = n_kvblk * LANES
    lo = _lo_lanes()

    @pl.when(bi == 0)
    def _():
        cos, sup, sdn = cos_ref[...], sup_ref[...], sdn_ref[...]
        qkv = jnp.dot(x_ref[...].astype(BF16), wqkv_ref[...], preferred_element_type=F32)
        for jq in range(n_qblk):
            blk = _rope_block(qkv[:, jq * LANES:(jq + 1) * LANES], cos, sup, sdn) * QK_SCALE
            h0, h1 = _stack_q_for_kv(blk, jq, lo)
            q_s[2 * jq] = h0.astype(BF16)
            q_s[2 * jq + 1] = h1.astype(BF16)
        for kb in range(n_kvblk):
            kn_ref[:, kb * LANES:(kb + 1) * LANES] = _rope_block(
                qkv[:, dq + kb * LANES:dq + (kb + 1) * LANES], cos, sup, sdn)
        vn_ref[...] = qkv[:, dq + dkv:]

    rows = pl.ds(pl.multiple_of(bi * t_new, t_new), t_new)
    heads_per_kvblk = 2 * n_qblk // n_kvblk
    for kb in range(n_kvblk):
        cols = slice(kb * LANES, (kb + 1) * LANES)
        q = jnp.concatenate([q_s[kb * heads_per_kvblk + i, rows, :]
                             for i in range(heads_per_kvblk)], axis=0)
        kt_old = kc_ref[0, cols, :].astype(BF16)
        vt_old = vc_ref[0, cols, :].astype(BF16)
        k_new = kn_ref[rows, cols].astype(BF16)
        v_new = vn_ref[rows, cols].astype(BF16)
        s_old = jnp.dot(q, kt_old, preferred_element_type=F32)
        s_new = lax.dot_general(q, k_new, _NT, preferred_element_type=F32)
        pv, denom = _softmax_pv_cached(s_old, s_new, vt_old, v_new, sink=sink_ref[kb])
        o = pv / denom
        for a in range(heads_per_kvblk // 2):
            jq = kb * (heads_per_kvblk // 2) + a
            att_s[rows, jq * LANES:(jq + 1) * LANES] = _unstack_o(o, a, t_new, lo).astype(BF16)

    @pl.when(bi == pl.num_programs(0) - 1)
    def _():
        mix = jnp.dot(att_s[...], wo_ref[...], preferred_element_type=F32)
        o_ref[...] = _layer_norm_rows(DEEPNORM_ALPHA * x_ref[...] + mix, g_ref[...], b_ref[...])


def _mix_b_step(x, w_qkv, w_o, rope, sink_col, k_cache, v_cache, g, b, *, bsz, t_new):
    n, d = x.shape
    dq = w_o.shape[0]
    dkv = (w_qkv.shape[1] - dq) // 2
    n_qblk, n_kvblk = dq // LANES, dkv // LANES
    lc = k_cache.shape[2]
    assert t_new % 16 == 0 and lc % LANES == 0
    cos, sup, sdn = rope
    full = lambda bi: (0, 0)
    return pl.pallas_call(
        functools.partial(_mix_b_step_kernel, t_new=t_new, n_qblk=n_qblk, n_kvblk=n_kvblk),
        out_shape=(jax.ShapeDtypeStruct((n, d), F32),
                   jax.ShapeDtypeStruct((n, dkv), F32),
                   jax.ShapeDtypeStruct((n, dkv), F32)),
        grid=(bsz,),
        in_specs=[
            _resident((n, d)),
            _resident(w_qkv.shape),
            _resident(w_o.shape),
            _resident(cos.shape), _resident(sup.shape), _resident(sdn.shape),
            _resident(sink_col.shape),
            pl.BlockSpec((1, dkv, lc), lambda bi: (bi, 0, 0)),
            pl.BlockSpec((1, dkv, lc), lambda bi: (bi, 0, 0)),
            _resident((1, d)),
            _resident((1, d)),
        ],
        out_specs=(pl.BlockSpec((n, d), full),
                   pl.BlockSpec((n, dkv), full),
                   pl.BlockSpec((n, dkv), full)),
        scratch_shapes=[
            pltpu.VMEM((2 * n_qblk, n, LANES), BF16),
            pltpu.VMEM((n, dq), BF16),
        ],
        compiler_params=_params(1),
        name="mix_b_step",
    )(x, w_qkv, w_o, cos, sup, sdn, sink_col, k_cache, v_cache, g, b)


_BIAS_EXT = 4 * LANES


def _rel_bias_kernel(ext_ref, bt_ref, bs_ref, *, t_new):
    hp = pl.program_id(0)
    win = (A_LEFT_CHUNKS + 2) * CHUNK
    near = 2 * A_REL_CLIP
    near0 = win - near
    r_loc = lax.broadcasted_iota(jnp.int32, (near, 1), 0)
    r_all = lax.broadcasted_iota(jnp.int32, (win, LANES), 0)
    c_all = lax.broadcasted_iota(jnp.int32, (win, LANES), 1)
    hidden = ((c_all < CHUNK) & (r_all >= win - CHUNK)) | ((c_all >= CHUNK) & (r_all < CHUNK))
    blocks = []
    for u in range(2):
        ext = ext_ref[pl.ds(2 * hp + u, 1), :]
        x = pltpu.roll(jnp.broadcast_to(ext, (near, _BIAS_EXT)), _BIAS_EXT - near, 1)
        for bit in range(near.bit_length() - 1):
            x = jnp.where((r_loc >> bit) & 1 == 1, pltpu.roll(x, 1 << bit, 1), x)
        far = jnp.broadcast_to(ext[:, near:near + LANES], (near0, LANES))
        g = jnp.concatenate([far, x[:, :LANES]], axis=0)
        bs_ref[0, u * t_new:(u + 1) * t_new, :] = g.T[:t_new]
        blocks.append(jnp.where(hidden, -jnp.inf, g))
    bt_ref[0] = jnp.concatenate(blocks, axis=1)


def _rel_bias_a(table, t_new):
    n_heads, n_rel = table.shape
    win = (A_LEFT_CHUNKS + 2) * CHUNK
    assert n_rel == 2 * A_REL_CLIP + 1 and n_heads % 2 == 0 and t_new % 8 == 0
    ext = jnp.concatenate(
        [table.astype(F32), jnp.broadcast_to(table[:, -1:].astype(F32), (n_heads, _BIAS_EXT - n_rel))], axis=1)
    return pl.pallas_call(
        functools.partial(_rel_bias_kernel, t_new=t_new),
        out_shape=(jax.ShapeDtypeStruct((n_heads // 2, win, 2 * PAIR), F32),
                   jax.ShapeDtypeStruct((n_heads // 2, 2 * t_new, win), F32)),
        grid=(n_heads // 2,),
        in_specs=[_resident(ext.shape)],
        out_specs=(pl.BlockSpec((1, win, 2 * PAIR), lambda i: (i, 0, 0)),
                   pl.BlockSpec((1, 2 * t_new, win), lambda i: (i, 0, 0))),
        compiler_params=_params(1),
        name="rel_bias_a",
    )(ext)


def _rope_tables(pos):
    half = HEAD_DIM // 2
    inv = ROPE_THETA ** (-jnp.arange(half, dtype=F32) / half)
    ang = pos.astype(F32)[:, None] * inv[None, :]
    cos, sin = jnp.cos(ang), jnp.sin(ang)
    zero = jnp.zeros_like(sin)
    tile = lambda a, c: jnp.concatenate([a, c] * (LANES // HEAD_DIM), axis=1)
    return tile(cos, cos), tile(zero, sin), tile(-sin, zero)


def _pair_masks_b(heads):
    win = (B_LEFT_CHUNKS + 2) * CHUNK
    k = jnp.arange(win)[:, None]
    q = jnp.arange(PAIR)[None, :]
    hidden = ((q < CHUNK) & (k >= win - CHUNK)) | ((q >= CHUNK) & (k < CHUNK))
    hidden_first = ((q < CHUNK) & (k >= CHUNK)) | ((q >= CHUNK) & (k >= PAIR))
    masks = jnp.where(jnp.stack([hidden, hidden_first]), -jnp.inf, 0.0).astype(F32)
    return jnp.tile(masks, (1, 1, heads))


def _sink_column(sinks, n_kvblk, rows):
    per = sinks.shape[0] // n_kvblk
    return jnp.repeat(sinks.astype(F32).reshape(n_kvblk, per), rows, axis=1)[..., None]


def kernel(x_prompt, x_sample, cache_a_k, cache_a_v, cache_b_k, cache_b_v, ln_g, ln_b,
           w_ffn_in, w_ffn_down, w_qkv_a, w_o_a, rel_bias_a, w_qkv_b, w_o_b, sinks_b):
    bsz, seq, d = x_prompt.shape
    sbsz, t_new, _ = x_sample.shape
    tm = 512
    xp = x_prompt.reshape(bsz * seq, d)
    xs = x_sample.reshape(sbsz * t_new, d)
    ln = lambda i, j: (ln_g[i, j][None], ln_b[i, j][None])
    feature_major = lambda c: jnp.transpose(c.reshape(c.shape[0], c.shape[1], -1), (0, 2, 1))

    w_in, w_down = w_ffn_in.astype(BF16), w_ffn_down.astype(BF16)

    def ffn(xp, xs, i, j):
        return (_ffn_ln(xp, w_in, w_down, *ln(i, 2 * j), layer=i, slot=j, tm=2 * tm),
                _ffn_ln(xs, w_in, w_down, *ln(i, 2 * j), layer=i, slot=j, tm=xs.shape[0]))

    xp, xs = ffn(xp, xs, 0, 0)
    wqkv_a, wo_a = w_qkv_a[0].astype(BF16), w_o_a[0].astype(BF16)
    d_a = wo_a.shape[0]
    bias_t, bias_step = _rel_bias_a(rel_bias_a[0], t_new)
    xp, ak_p, av_p = _mix_a_prompt(xp, wqkv_a, wo_a, bias_t, *ln(0, 1), bsz=bsz, seq=seq, tm=tm)
    xs, ak_s, av_s = _mix_a_step(xs, wqkv_a, wo_a, bias_step,
                                 feature_major(cache_a_k[0]), feature_major(cache_a_v[0]),
                                 *ln(0, 1), bsz=sbsz, t_new=t_new)
    xp, xs = ffn(xp, xs, 0, 1)

    xp, xs = ffn(xp, xs, 1, 0)
    wqkv_b, wo_b = w_qkv_b[0].astype(BF16), w_o_b[0].astype(BF16)
    dkv = (wqkv_b.shape[1] - wo_b.shape[0]) // 2
    n_kvblk = dkv // LANES
    heads_per_kvblk = sinks_b.shape[1] // n_kvblk
    keep_b = min((B_LEFT_CHUNKS) * CHUNK, seq)
    xp, bk_p, bv_p = _mix_b_prompt(
        xp, wqkv_b, wo_b, _rope_tables(jnp.arange(seq)), _pair_masks_b(heads_per_kvblk),
        jnp.transpose(_sink_column(sinks_b[0], n_kvblk, PAIR), (0, 2, 1)),
        *ln(1, 1), bsz=bsz, seq=seq, tm=tm, keep=keep_b)
    step_pos = jnp.tile(PAST_LEN + jnp.arange(t_new), sbsz)
    xs, bk_s, bv_s = _mix_b_step(
        xs, wqkv_b, wo_b, _rope_tables(step_pos), _sink_column(sinks_b[0], n_kvblk, t_new),
        feature_major(cache_b_k[0]), feature_major(cache_b_v[0]),
        *ln(1, 1), bsz=sbsz, t_new=t_new)
    xp, xs = ffn(xp, xs, 1, 1)

    heads_a = d_a // HEAD_DIM
    heads_kv = dkv // HEAD_DIM
    return (xp.reshape(bsz, seq, d), xs.reshape(sbsz, t_new, d),
            ak_p.reshape(1, bsz, -1, heads_a, HEAD_DIM), av_p.reshape(1, bsz, -1, heads_a, HEAD_DIM),
            bk_p.reshape(1, bsz, -1, heads_kv, HEAD_DIM), bv_p.reshape(1, bsz, -1, heads_kv, HEAD_DIM),
            ak_s.reshape(1, sbsz, t_new, heads_a, HEAD_DIM), av_s.reshape(1, sbsz, t_new, heads_a, HEAD_DIM),
            bk_s.reshape(1, sbsz, t_new, heads_kv, HEAD_DIM), bv_s.reshape(1, sbsz, t_new, heads_kv, HEAD_DIM))
```

```python
import functools

import jax
import jax.numpy as jnp
from jax import lax
from jax.experimental import pallas as pl
from jax.experimental.pallas import tpu as pltpu

F32 = jnp.float32
BF16 = jnp.bfloat16

DEPTH = 2
CHUNK = 64
HEAD_DIM = 64
A_LEFT_CHUNKS = 8
A_REL_CLIP = 128
B_LEFT_CHUNKS = 2
PAST_LEN = 2048
ROPE_THETA = 10000.0
LN_EPS = 1e-5
DEEPNORM_ALPHA = (2.0 * DEPTH) ** 0.25
QK_SCALE = HEAD_DIM ** -0.5
LOG2E = 1.4426950408889634
QK_SCALE_LOG2 = QK_SCALE * LOG2E

LANES = 128
PAIR = 2 * CHUNK
MXU_N = 256
FFN_ROW_GROUP = 256
V7X_VMEM_LIMIT_BYTES = 56 * 1024 * 1024

_NT = (((1,), (1,)), ((), ()))


def _params(n_axes):
    return pltpu.CompilerParams(
        dimension_semantics=("arbitrary",) * n_axes,
        vmem_limit_bytes=V7X_VMEM_LIMIT_BYTES)


def _resident(shape):
    nd = len(shape)
    return pl.BlockSpec(shape, lambda *_: (0,) * nd, pipeline_mode=pl.Buffered(1))


def _layer_norm_rows(y, g, b):
    mu = jnp.mean(y, axis=-1, keepdims=True)
    d = y - mu
    var = jnp.mean(d * d, axis=-1, keepdims=True)
    return d * lax.rsqrt(var + LN_EPS) * g + b


def _lo_lanes():
    return lax.broadcasted_iota(jnp.int32, (1, LANES), 1) < HEAD_DIM


def _ffn_ln_kernel(xp_ref, xs_ref, win_ref, wdown_ref, g_ref, b_ref, op_ref, os_ref, *, d_ff, n_steps):
    def run(x_ref, o_ref):
        tm = x_ref.shape[0]
        group = min(tm, FFN_ROW_GROUP)
        groups = [slice(r, r + group) for r in range(0, tm, group)]

        def hidden(rows):
            xb = x_ref[rows, :].astype(BF16)
            gate = jnp.dot(xb, win_ref[:, :d_ff], preferred_element_type=F32)
            up = jnp.dot(xb, win_ref[:, d_ff:], preferred_element_type=F32)
            return ((gate * jax.nn.sigmoid(gate)) * up).astype(BF16)

        def finish(rows, act):
            y = jnp.dot(act, wdown_ref[...], preferred_element_type=F32)
            o_ref[rows, :] = _layer_norm_rows(DEEPNORM_ALPHA * x_ref[rows, :] + 0.5 * y,
                                              g_ref[...], b_ref[...])

        act = hidden(groups[0])
        for i, rows in enumerate(groups):
            act_next = hidden(groups[i + 1]) if i + 1 < len(groups) else None
            finish(rows, act)
            act = act_next

    step = pl.program_id(0)

    @pl.when(step < n_steps)
    def _():
        run(xp_ref, op_ref)

    @pl.when(step == n_steps)
    def _():
        run(xs_ref, os_ref)


def _ffn_ln(xp, xs, w_in, w_down, g, b, *, layer, slot, tm):
    n, d = xp.shape
    ns = xs.shape[0]
    d_ff = w_down.shape[2]
    assert n % tm == 0 and ns % 8 == 0
    n_steps = n // tm
    pick = lambda i: (layer, slot, 0, 0)
    prompt_tile = lambda i: (jnp.minimum(i, n_steps - 1), 0)
    return pl.pallas_call(
        functools.partial(_ffn_ln_kernel, d_ff=d_ff, n_steps=n_steps),
        out_shape=(jax.ShapeDtypeStruct((n, d), F32), jax.ShapeDtypeStruct((ns, d), F32)),
        grid=(n_steps + 1,),
        in_specs=[
            pl.BlockSpec((tm, d), prompt_tile),
            _resident((ns, d)),
            pl.BlockSpec((None, None, d, 2 * d_ff), pick, pipeline_mode=pl.Buffered(1)),
            pl.BlockSpec((None, None, d_ff, d), pick, pipeline_mode=pl.Buffered(1)),
            _resident((1, d)),
            _resident((1, d)),
        ],
        out_specs=(pl.BlockSpec((tm, d), prompt_tile),
                   pl.BlockSpec((ns, d), lambda i: (0, 0))),
        compiler_params=_params(1),
        name="ffn_ln",
    )(xp, xs, w_in, w_down, g, b)


def _split_heads(q2, lo):
    zero = jnp.zeros_like(q2)
    return jnp.concatenate([jnp.where(lo, q2, zero), jnp.where(lo, zero, q2)], axis=0)


def _project_out_rows(att_s, x_ref, wo_ref, g_ref, b_ref, o_ref, n_blk):
    tm = x_ref.shape[0]
    group = min(tm, FFN_ROW_GROUP)
    mixes = []
    for r in range(0, tm + group, group):
        if r < tm:
            att = jnp.concatenate([att_s[j, r:r + group, :] for j in range(n_blk)], axis=1)
            mixes.append(jnp.dot(att, wo_ref[...], preferred_element_type=F32))
        if r >= group:
            rows = slice(r - group, r)
            o_ref[rows, :] = _layer_norm_rows(DEEPNORM_ALPHA * x_ref[rows, :] + mixes[r // group - 1],
                                              g_ref[...], b_ref[...])


def _store_vt(vt_s, blk_idx, tile_blk0, v_blk):
    vt = v_blk.T
    for c in range(v_blk.shape[0] // LANES):
        vt_s[blk_idx, tile_blk0 + c] = vt[:, c * LANES:(c + 1) * LANES].astype(BF16)


def _vt_window(vt_s, blk_idx, kblk, nkeys):
    return jnp.concatenate([vt_s[blk_idx, kblk + i] for i in range(nkeys // LANES)], axis=1)


def _mix_a_prompt_kernel(x_ref, wqkv_ref, wo_ref, bias_ref, g_ref, b_ref,
                         o_ref, kc_ref, vc_ref,
                         q_s, k_s, vt_s, att_s, *, tm, n_blk):
    t = pl.program_id(1)
    x = x_ref[...]
    xb = x.astype(BF16)
    row0 = pl.multiple_of(t * tm, tm)
    tile_blk0 = t * (tm // LANES)
    d_attn = n_blk * LANES
    for j in range(3 * d_attn // MXU_N):
        r = jnp.dot(xb, wqkv_ref[:, j * MXU_N:(j + 1) * MXU_N], preferred_element_type=F32)
        sec, jj = divmod(j, d_attn // MXU_N)
        for u in range(MXU_N // LANES):
            blk = r[:, u * LANES:(u + 1) * LANES]
            hp = jj * (MXU_N // LANES) + u
            if sec == 0:
                q_s[hp] = (blk * QK_SCALE_LOG2).astype(BF16)
            elif sec == 1:
                k_s[hp, pl.ds(row0, tm), :] = blk.astype(BF16)
                kc_ref[0, :, hp * LANES:(hp + 1) * LANES] = blk
            else:
                _store_vt(vt_s, hp, tile_blk0, blk)
                vc_ref[0, :, hp * LANES:(hp + 1) * LANES] = blk

    lo = _lo_lanes()
    win = (A_LEFT_CHUNKS + 2) * CHUNK
    hp_per_iter = 4

    def scores(hp, cp0, kblk, span, subs):
        qq = jnp.concatenate([_split_heads(q_s[hp, (cp0 + i) * PAIR:(cp0 + i + 1) * PAIR, :], lo)
                              for i in range(2)], axis=0)
        kw = k_s[hp, pl.ds(pl.multiple_of(kblk * LANES, LANES), span), :]
        s = lax.dot_general(kw, qq, _NT, preferred_element_type=F32)
        parts = []
        for i, (off, nk) in enumerate(subs):
            segs = []
            for lo_r, hi_r, biased in ((0, CHUNK, True), (CHUNK, win - 2 * A_REL_CLIP, False),
                                       (win - 2 * A_REL_CLIP, win, True)):
                lo_r, hi_r = max(lo_r, win - nk), hi_r
                if lo_r >= hi_r:
                    continue
                r0 = off + lo_r - (win - nk)
                seg = s[r0:r0 + hi_r - lo_r, i * 2 * PAIR:(i + 1) * 2 * PAIR]
                segs.append(seg + bias_ref[hp, lo_r:hi_r, :] if biased else seg)
            parts.append(jnp.concatenate(segs, axis=0))
        return parts

    def finish(hp, cp0, kblk, span, subs, s_parts):
        e_cols, denoms = [], []
        for (off, nk), s in zip(subs, s_parts):
            m = jnp.max(s, axis=0, keepdims=True)
            e = jnp.exp2(s - m)
            denoms.append(jnp.sum(e, axis=0, keepdims=True))
            e = e.astype(BF16)
            pads = [jnp.zeros((n, 2 * PAIR), BF16) for n in (off, span - off - nk)]
            e_cols.append(jnp.concatenate([p for p in (pads[0], e, pads[1]) if p.shape[0]], axis=0))
        o_t = jnp.dot(_vt_window(vt_s, hp, kblk, span), jnp.concatenate(e_cols, axis=1),
                      preferred_element_type=F32)
        for i, denom in enumerate(denoms):
            o_i = o_t[:, i * 2 * PAIR:(i + 1) * 2 * PAIR] / denom
            both = jnp.concatenate([o_i[:HEAD_DIM, :PAIR], o_i[HEAD_DIM:, PAIR:]], axis=0)
            att_s[hp, (cp0 + i) * PAIR:(cp0 + i + 1) * PAIR, :] = both.T.astype(BF16)

    def pipelined(units):
        s = scores(*units[0])
        for i, u in enumerate(units):
            s_next = scores(*units[i + 1]) if i + 1 < len(units) else None
            finish(*u, s)
            s = s_next

    n_cp = tm // PAIR
    assert n_cp % 2 == 0

    @pl.when(t == 0)
    def _():
        def body(hp, c):
            pipelined([(hp, cp0, 0, (cp0 + 2) * PAIR, [(0, (cp0 + 1) * PAIR), (0, (cp0 + 2) * PAIR)])
                       for cp0 in range(0, n_cp, 2)])
            return c
        lax.fori_loop(0, n_blk, body, 0)

    @pl.when(t > 0)
    def _():
        def body(i, c):
            pipelined([(i * hp_per_iter + dh, cp0, tile_blk0 + cp0 + 1 - win // LANES, win + PAIR,
                        [(0, win), (PAIR, win)])
                       for dh in range(hp_per_iter) for cp0 in range(0, n_cp, 2)])
            return c
        lax.fori_loop(0, n_blk // hp_per_iter, body, 0)

    _project_out_rows(att_s, x_ref, wo_ref, g_ref, b_ref, o_ref, n_blk)


def _mix_a_prompt(x, w_qkv, w_o, bias2, g, b, *, bsz, seq, tm):
    n, d = x.shape
    d_attn = w_o.shape[0]
    n_blk = d_attn // LANES
    keep = min(A_LEFT_CHUNKS * CHUNK, seq)
    assert seq % tm == 0 and keep == tm and tm >= A_LEFT_CHUNKS * CHUNK and tm % PAIR == 0
    nt = seq // tm
    return pl.pallas_call(
        functools.partial(_mix_a_prompt_kernel, tm=tm, n_blk=n_blk),
        out_shape=(jax.ShapeDtypeStruct((n, d), F32),
                   jax.ShapeDtypeStruct((bsz, keep, d_attn), F32),
                   jax.ShapeDtypeStruct((bsz, keep, d_attn), F32)),
        grid=(bsz, nt),
        in_specs=[
            pl.BlockSpec((tm, d), lambda bi, t: (bi * nt + t, 0)),
            _resident(w_qkv.shape),
            _resident(w_o.shape),
            _resident(bias2.shape),
            _resident((1, d)),
            _resident((1, d)),
        ],
        out_specs=(pl.BlockSpec((tm, d), lambda bi, t: (bi * nt + t, 0)),
                   pl.BlockSpec((1, keep, d_attn), lambda bi, t: (bi, 0, 0)),
                   pl.BlockSpec((1, keep, d_attn), lambda bi, t: (bi, 0, 0))),
        scratch_shapes=[
            pltpu.VMEM((n_blk, tm, LANES), BF16),
            pltpu.VMEM((n_blk, seq, LANES), BF16),
            pltpu.VMEM((n_blk, seq // LANES, LANES, LANES), BF16),
            pltpu.VMEM((n_blk, tm, LANES), BF16),
        ],
        compiler_params=_params(2),
        name="mix_a_prompt",
    )(x, w_qkv, w_o, bias2, g, b)


def _softmax_pv_cached(s_old, s_new, vt_old, v_new, sink=None):
    m = jnp.maximum(jnp.max(s_old, axis=1, keepdims=True), jnp.max(s_new, axis=1, keepdims=True))
    if sink is not None:
        m = jnp.maximum(m, sink)
    e_old = jnp.exp(s_old - m)
    e_new = jnp.exp(s_new - m)
    denom = jnp.sum(e_old, axis=1, keepdims=True) + jnp.sum(e_new, axis=1, keepdims=True)
    if sink is not None:
        denom = denom + jnp.exp(sink - m)
    pv = (lax.dot_general(e_old.astype(BF16), vt_old, _NT, preferred_element_type=F32)
          + jnp.dot(e_new.astype(BF16), v_new, preferred_element_type=F32))
    return pv, denom


def _mix_a_step_kernel(x_ref, wqkv_ref, wo_ref, bias_ref, kct_ref, vct_ref, g_ref, b_ref,
                       o_ref, kn_ref, vn_ref, qkv_s, att_s, *, t_new, n_blk):
    bi = pl.program_id(0)
    d_attn = n_blk * LANES
    lc = kct_ref.shape[2]

    @pl.when(bi == 0)
    def _():
        qkv = jnp.dot(x_ref[...].astype(BF16), wqkv_ref[...], preferred_element_type=F32)
        qkv_s[...] = qkv
        kn_ref[...] = qkv[:, d_attn:2 * d_attn]
        vn_ref[...] = qkv[:, 2 * d_attn:]

    lo = _lo_lanes()
    rows = pl.ds(pl.multiple_of(bi * t_new, t_new), t_new)
    for hp in range(n_blk):
        cols = slice(hp * LANES, (hp + 1) * LANES)
        q2 = (qkv_s[rows, cols] * QK_SCALE).astype(BF16)
        kt_old = kct_ref[0, cols, :].astype(BF16)
        vt_old = vct_ref[0, cols, :].astype(BF16)
        k_new = qkv_s[rows, d_attn + hp * LANES:d_attn + (hp + 1) * LANES].astype(BF16)
        v_new = qkv_s[rows, 2 * d_attn + hp * LANES:2 * d_attn + (hp + 1) * LANES].astype(BF16)
        qq = _split_heads(q2, lo)
        bias = bias_ref[hp]
        s_old = jnp.dot(qq, kt_old, preferred_element_type=F32) + bias[:, :lc]
        s_new = lax.dot_general(qq, k_new, _NT, preferred_element_type=F32) + bias[:, lc:lc + t_new]
        pv, denom = _softmax_pv_cached(s_old, s_new, vt_old, v_new)
        o = pv / denom
        att_s[rows, cols] = jnp.where(lo, o[:t_new], o[t_new:]).astype(BF16)

    @pl.when(bi == pl.num_programs(0) - 1)
    def _():
        mix = jnp.dot(att_s[...], wo_ref[...], preferred_element_type=F32)
        o_ref[...] = _layer_norm_rows(DEEPNORM_ALPHA * x_ref[...] + mix, g_ref[...], b_ref[...])


def _mix_a_step(x, w_qkv, w_o, bias2, k_cache, v_cache, g, b, *, bsz, t_new):
    n, d = x.shape
    d_attn = w_o.shape[0]
    n_blk = d_attn // LANES
    lc = k_cache.shape[2]
    assert lc == A_LEFT_CHUNKS * CHUNK and t_new <= CHUNK and t_new % 16 == 0
    full = lambda bi: (0, 0)
    return pl.pallas_call(
        functools.partial(_mix_a_step_kernel, t_new=t_new, n_blk=n_blk),
        out_shape=(jax.ShapeDtypeStruct((n, d), F32),
                   jax.ShapeDtypeStruct((n, d_attn), F32),
                   jax.ShapeDtypeStruct((n, d_attn), F32)),
        grid=(bsz,),
        in_specs=[
            _resident((n, d)),
            _resident(w_qkv.shape),
            _resident(w_o.shape),
            _resident(bias2.shape),
            pl.BlockSpec((1, d_attn, lc), lambda bi: (bi, 0, 0)),
            pl.BlockSpec((1, d_attn, lc), lambda bi: (bi, 0, 0)),
            _resident((1, d)),
            _resident((1, d)),
        ],
        out_specs=(pl.BlockSpec((n, d), full),
                   pl.BlockSpec((n, d_attn), full),
                   pl.BlockSpec((n, d_attn), full)),
        scratch_shapes=[
            pltpu.VMEM((n, 3 * d_attn), F32),
            pltpu.VMEM((n, d_attn), BF16),
        ],
        compiler_params=_params(1),
        name="mix_a_step",
    )(x, w_qkv, w_o, bias2, k_cache, v_cache, g, b)


def _rope_block(blk, cos, sin_up, sin_dn):
    half = HEAD_DIM // 2
    return blk * cos + pltpu.roll(blk, half, 1) * sin_up + pltpu.roll(blk, LANES - half, 1) * sin_dn


def _stack_q_for_kv(blk, j, lo):
    zero = jnp.zeros_like(blk)
    rolled = pltpu.roll(blk, HEAD_DIM, 1)
    if (j // 2) % 2 == 0:
        return jnp.where(lo, blk, zero), jnp.where(lo, rolled, zero)
    return jnp.where(lo, zero, rolled), jnp.where(lo, zero, blk)


def _unstack_o(o, i_pair, rows, lo):
    i0, i1 = 2 * i_pair, 2 * i_pair + 1
    o0 = o[i0 * rows:(i0 + 1) * rows]
    o1 = o[i1 * rows:(i1 + 1) * rows]
    if i0 // 4 == 0:
        o1 = pltpu.roll(o1, HEAD_DIM, 1)
    else:
        o0 = pltpu.roll(o0, HEAD_DIM, 1)
    return jnp.where(lo, o0, o1)


def _mix_b_prompt_kernel(x_ref, wqkv_ref, wo_ref, cos_ref, sup_ref, sdn_ref, mask_ref, sink_ref,
                         g_ref, b_ref, o_ref, kc_ref, vc_ref,
                         q_s, k_s, vt_s, att_s, *, tm, n_qblk, n_kvblk, keep):
    t = pl.program_id(1)
    x = x_ref[...]
    xb = x.astype(BF16)
    row0 = pl.multiple_of(t * tm, tm)
    tile_blk0 = t * (tm // LANES)
    lo = _lo_lanes()
    cos, sup, sdn = cos_ref[...], sup_ref[...], sdn_ref[...]
    dq = n_qblk * LANES
    dkv = n_kvblk * LANES
    for j in range((dq + 2 * dkv) // MXU_N):
        r = jnp.dot(xb, wqkv_ref[:, j * MXU_N:(j + 1) * MXU_N], preferred_element_type=F32)
        for u in range(MXU_N // LANES):
            blk = r[:, u * LANES:(u + 1) * LANES]
            col = j * MXU_N + u * LANES
            if col < dq:
                jq = col // LANES
                h0, h1 = _stack_q_for_kv(_rope_block(blk, cos, sup, sdn) * QK_SCALE_LOG2, jq, lo)
                q_s[2 * jq] = h0.astype(BF16)
                q_s[2 * jq + 1] = h1.astype(BF16)
            elif col < dq + dkv:
                kb = (col - dq) // LANES
                kr = _rope_block(blk, cos, sup, sdn)
                k_s[kb, pl.ds(row0, tm), :] = kr.astype(BF16)
                kc_ref[0, :, kb * LANES:(kb + 1) * LANES] = kr[tm - keep:]
            else:
                kb = (col - dq - dkv) // LANES
                _store_vt(vt_s, kb, tile_blk0, blk)
                vc_ref[0, :, kb * LANES:(kb + 1) * LANES] = blk[tm - keep:]

    win = (B_LEFT_CHUNKS + 2) * CHUNK
    heads_per_kvblk = 2 * n_qblk // n_kvblk

    first_tile = (t == 0).astype(jnp.int32)

    def window(cp):
        kblk = tile_blk0 + cp + 1 - win // LANES
        if cp == 0:
            return jnp.maximum(kblk, 0), mask_ref[first_tile]
        return kblk, mask_ref[0]

    def scores(kb, cp):
        q = q_s[kb * heads_per_kvblk:(kb + 1) * heads_per_kvblk, cp * PAIR:(cp + 1) * PAIR, :]
        q = q.reshape(heads_per_kvblk * PAIR, LANES)
        kblk, mask = window(cp)
        kw = k_s[kb, pl.ds(pl.multiple_of(kblk * LANES, LANES), win), :]
        return lax.dot_general(kw, q, _NT, preferred_element_type=F32) + mask

    def finish(kb, cp, s):
        sink = sink_ref[kb] * LOG2E
        m = jnp.maximum(jnp.max(s, axis=0, keepdims=True), sink)
        e = jnp.exp2(s - m)
        denom = jnp.sum(e, axis=0, keepdims=True) + jnp.exp2(sink - m)
        o_t = jnp.dot(_vt_window(vt_s, kb, window(cp)[0], win), e.astype(BF16),
                      preferred_element_type=F32) / denom
        for a in range(heads_per_kvblk // 2):
            parts = []
            for i in (2 * a, 2 * a + 1):
                r0 = HEAD_DIM * (i // (heads_per_kvblk // 2))
                parts.append(o_t[r0:r0 + HEAD_DIM, i * PAIR:(i + 1) * PAIR])
            jq = kb * (heads_per_kvblk // 2) + a
            att_s[jq, cp * PAIR:(cp + 1) * PAIR, :] = jnp.concatenate(parts, axis=0).T.astype(BF16)

    units = [(kb, cp) for kb in range(n_kvblk) for cp in range(tm // PAIR)]
    s = scores(*units[0])
    for i, u in enumerate(units):
        s_next = scores(*units[i + 1]) if i + 1 < len(units) else None
        finish(*u, s)
        s = s_next

    _project_out_rows(att_s, x_ref, wo_ref, g_ref, b_ref, o_ref, n_qblk)


def _mix_b_prompt(x, w_qkv, w_o, rope, mask, sink_row, g, b, *, bsz, seq, tm, keep):
    n, d = x.shape
    dq = w_o.shape[0]
    dkv = (w_qkv.shape[1] - dq) // 2
    n_qblk, n_kvblk = dq // LANES, dkv // LANES
    assert seq % tm == 0 and keep <= tm and tm % PAIR == 0 and tm >= (B_LEFT_CHUNKS + 2) * CHUNK
    assert (2 * n_qblk) % n_kvblk == 0 and (2 * n_qblk // n_kvblk) % 4 == 0
    nt = seq // tm
    cos, sup, sdn = rope
    rope_spec = pl.BlockSpec((tm, LANES), lambda bi, t: (t, 0))
    return pl.pallas_call(
        functools.partial(_mix_b_prompt_kernel, tm=tm, n_qblk=n_qblk, n_kvblk=n_kvblk, keep=keep),
        out_shape=(jax.ShapeDtypeStruct((n, d), F32),
                   jax.ShapeDtypeStruct((bsz, keep, dkv), F32),
                   jax.ShapeDtypeStruct((bsz, keep, dkv), F32)),
        grid=(bsz, nt),
        in_specs=[
            pl.BlockSpec((tm, d), lambda bi, t: (bi * nt + t, 0)),
            _resident(w_qkv.shape),
            _resident(w_o.shape),
            rope_spec, rope_spec, rope_spec,
            _resident(mask.shape),
            _resident(sink_row.shape),
            _resident((1, d)),
            _resident((1, d)),
        ],
        out_specs=(pl.BlockSpec((tm, d), lambda bi, t: (bi * nt + t, 0)),
                   pl.BlockSpec((1, keep, dkv), lambda bi, t: (bi, 0, 0)),
                   pl.BlockSpec((1, keep, dkv), lambda bi, t: (bi, 0, 0))),
        scratch_shapes=[
            pltpu.VMEM((2 * n_qblk, tm, LANES), BF16),
            pltpu.VMEM((n_kvblk, seq, LANES), BF16),
            pltpu.VMEM((n_kvblk, seq // LANES, LANES, LANES), BF16),
            pltpu.VMEM((n_qblk, tm, LANES), BF16),
        ],
        compiler_params=_params(2),
        name="mix_b_prompt",
    )(x, w_qkv, w_o, cos, sup, sdn, mask, sink_row, g, b)


def _mix_b_step_kernel(x_ref, wqkv_ref, wo_ref, cos_ref, sup_ref, sdn_ref, sink_ref,
                       kc_ref, vc_ref, g_ref, b_ref, o_ref, kn_ref, vn_ref,
                       q_s, att_s, *, t_new, n_qblk, n_kvblk):
    bi = pl.program_id(0)
    dq = n_qblk * LANES
    dkv = n_kvblk * LANES
    lo = _lo_lanes()

    @pl.when(bi == 0)
    def _():
        cos, sup, sdn = cos_ref[...], sup_ref[...], sdn_ref[...]
        qkv = jnp.dot(x_ref[...].astype(BF16), wqkv_ref[...], preferred_element_type=F32)
        for jq in range(n_qblk):
            blk = _rope_block(qkv[:, jq * LANES:(jq + 1) * LANES], cos, sup, sdn) * QK_SCALE
            h0, h1 = _stack_q_for_kv(blk, jq, lo)
            q_s[2 * jq] = h0.astype(BF16)
            q_s[2 * jq + 1] = h1.astype(BF16)
        for kb in range(n_kvblk):
            kn_ref[:, kb * LANES:(kb + 1) * LANES] = _rope_block(
                qkv[:, dq + kb * LANES:dq + (kb + 1) * LANES], cos, sup, sdn)
        vn_ref[...] = qkv[:, dq + dkv:]

    rows = pl.ds(pl.multiple_of(bi * t_new, t_new), t_new)
    heads_per_kvblk = 2 * n_qblk // n_kvblk
    for kb in range(n_kvblk):
        cols = slice(kb * LANES, (kb + 1) * LANES)
        q = jnp.concatenate([q_s[kb * heads_per_kvblk + i, rows, :]
                             for i in range(heads_per_kvblk)], axis=0)
        kt_old = kc_ref[0, cols, :].astype(BF16)
        vt_old = vc_ref[0, cols, :].astype(BF16)
        k_new = kn_ref[rows, cols].astype(BF16)
        v_new = vn_ref[rows, cols].astype(BF16)
        s_old = jnp.dot(q, kt_old, preferred_element_type=F32)
        s_new = lax.dot_general(q, k_new, _NT, preferred_element_type=F32)
        pv, denom = _softmax_pv_cached(s_old, s_new, vt_old, v_new, sink=sink_ref[kb])
        o = pv / denom
        for a in range(heads_per_kvblk // 2):
            jq = kb * (heads_per_kvblk // 2) + a
            att_s[rows, jq * LANES:(jq + 1) * LANES] = _unstack_o(o, a, t_new, lo).astype(BF16)

    @pl.when(bi == pl.num_programs(0) - 1)
    def _():
        mix = jnp.dot(att_s[...], wo_ref[...], preferred_element_type=F32)
        o_ref[...] = _layer_norm_rows(DEEPNORM_ALPHA * x_ref[...] + mix, g_ref[...], b_ref[...])


def _mix_b_step(x, w_qkv, w_o, rope, sink_col, k_cache, v_cache, g, b, *, bsz, t_new):
    n, d = x.shape
    dq = w_o.shape[0]
    dkv = (w_qkv.shape[1] - dq) // 2
    n_qblk, n_kvblk = dq // LANES, dkv // LANES
    lc = k_cache.shape[2]
    assert t_new % 16 == 0 and lc % LANES == 0
    cos, sup, sdn = rope
    full = lambda bi: (0, 0)
    return pl.pallas_call(
        functools.partial(_mix_b_step_kernel, t_new=t_new, n_qblk=n_qblk, n_kvblk=n_kvblk),
        out_shape=(jax.ShapeDtypeStruct((n, d), F32),
                   jax.ShapeDtypeStruct((n, dkv), F32),
                   jax.ShapeDtypeStruct((n, dkv), F32)),
        grid=(bsz,),
        in_specs=[
            _resident((n, d)),
            _resident(w_qkv.shape),
            _resident(w_o.shape),
            _resident(cos.shape), _resident(sup.shape), _resident(sdn.shape),
            _resident(sink_col.shape),
            pl.BlockSpec((1, dkv, lc), lambda bi: (bi, 0, 0)),
            pl.BlockSpec((1, dkv, lc), lambda bi: (bi, 0, 0)),
            _resident((1, d)),
            _resident((1, d)),
        ],
        out_specs=(pl.BlockSpec((n, d), full),
                   pl.BlockSpec((n, dkv), full),
                   pl.BlockSpec((n, dkv), full)),
        scratch_shapes=[
            pltpu.VMEM((2 * n_qblk, n, LANES), BF16),
            pltpu.VMEM((n, dq), BF16),
        ],
        compiler_params=_params(1),
        name="mix_b_step",
    )(x, w_qkv, w_o, cos, sup, sdn, sink_col, k_cache, v_cache, g, b)


_BIAS_EXT = 4 * LANES


def _rel_bias_kernel(ext_ref, bt_ref, bs_ref, *, t_new):
    hp = pl.program_id(0)
    win = (A_LEFT_CHUNKS + 2) * CHUNK
    near = 2 * A_REL_CLIP
    near0 = win - near
    r_loc = lax.broadcasted_iota(jnp.int32, (near, 1), 0)
    r_all = lax.broadcasted_iota(jnp.int32, (win, LANES), 0)
    c_all = lax.broadcasted_iota(jnp.int32, (win, LANES), 1)
    hidden = ((c_all < CHUNK) & (r_all >= win - CHUNK)) | ((c_all >= CHUNK) & (r_all < CHUNK))
    blocks = []
    for u in range(2):
        ext = ext_ref[pl.ds(2 * hp + u, 1), :]
        x = pltpu.roll(jnp.broadcast_to(ext, (near, _BIAS_EXT)), _BIAS_EXT - near, 1)
        for bit in range(near.bit_length() - 1):
            x = jnp.where((r_loc >> bit) & 1 == 1, pltpu.roll(x, 1 << bit, 1), x)
        far = jnp.broadcast_to(ext[:, near:near + LANES], (near0, LANES))
        g = jnp.concatenate([far, x[:, :LANES]], axis=0)
        bs_ref[0, u * t_new:(u + 1) * t_new, :] = g.T[:t_new]
        blocks.append(jnp.where(hidden, -jnp.inf, (g - ext[:, near:near + 1]) * LOG2E))
    bt_ref[0] = jnp.concatenate(blocks, axis=1)


def _rel_bias_a(table, t_new):
    n_heads, n_rel = table.shape
    win = (A_LEFT_CHUNKS + 2) * CHUNK
    assert n_rel == 2 * A_REL_CLIP + 1 and n_heads % 2 == 0 and t_new % 8 == 0
    ext = jnp.concatenate(
        [table.astype(F32), jnp.broadcast_to(table[:, -1:].astype(F32), (n_heads, _BIAS_EXT - n_rel))], axis=1)
    return pl.pallas_call(
        functools.partial(_rel_bias_kernel, t_new=t_new),
        out_shape=(jax.ShapeDtypeStruct((n_heads // 2, win, 2 * PAIR), F32),
                   jax.ShapeDtypeStruct((n_heads // 2, 2 * t_new, win), F32)),
        grid=(n_heads // 2,),
        in_specs=[_resident(ext.shape)],
        out_specs=(pl.BlockSpec((1, win, 2 * PAIR), lambda i: (i, 0, 0)),
                   pl.BlockSpec((1, 2 * t_new, win), lambda i: (i, 0, 0))),
        compiler_params=_params(1),
        name="rel_bias_a",
    )(ext)


def _rope_tables(pos):
    half = HEAD_DIM // 2
    inv = ROPE_THETA ** (-jnp.arange(half, dtype=F32) / half)
    ang = pos.astype(F32)[:, None] * inv[None, :]
    cos, sin = jnp.cos(ang), jnp.sin(ang)
    zero = jnp.zeros_like(sin)
    tile = lambda a, c: jnp.concatenate([a, c] * (LANES // HEAD_DIM), axis=1)
    return tile(cos, cos), tile(zero, sin), tile(-sin, zero)


def _pair_masks_b(heads):
    win = (B_LEFT_CHUNKS + 2) * CHUNK
    k = jnp.arange(win)[:, None]
    q = jnp.arange(PAIR)[None, :]
    hidden = ((q < CHUNK) & (k >= win - CHUNK)) | ((q >= CHUNK) & (k < CHUNK))
    hidden_first = ((q < CHUNK) & (k >= CHUNK)) | ((q >= CHUNK) & (k >= PAIR))
    masks = jnp.where(jnp.stack([hidden, hidden_first]), -jnp.inf, 0.0).astype(F32)
    return jnp.tile(masks, (1, 1, heads))


def _sink_column(sinks, n_kvblk, rows):
    per = sinks.shape[0] // n_kvblk
    return jnp.repeat(sinks.astype(F32).reshape(n_kvblk, per), rows, axis=1)[..., None]


def kernel(x_prompt, x_sample, cache_a_k, cache_a_v, cache_b_k, cache_b_v, ln_g, ln_b,
           w_ffn_in, w_ffn_down, w_qkv_a, w_o_a, rel_bias_a, w_qkv_b, w_o_b, sinks_b):
    bsz, seq, d = x_prompt.shape
    sbsz, t_new, _ = x_sample.shape
    tm = 512
    xp = x_prompt.reshape(bsz * seq, d)
    xs = x_sample.reshape(sbsz * t_new, d)
    ln = lambda i, j: (ln_g[i, j][None], ln_b[i, j][None])
    feature_major = lambda c: jnp.transpose(c.reshape(c.shape[0], c.shape[1], -1), (0, 2, 1))

    w_in, w_down = w_ffn_in.astype(BF16), w_ffn_down.astype(BF16)

    def ffn(xp, xs, i, j):
        return _ffn_ln(xp, xs, w_in, w_down, *ln(i, 2 * j), layer=i, slot=j, tm=2 * tm)

    xp, xs = ffn(xp, xs, 0, 0)
    wqkv_a, wo_a = w_qkv_a[0].astype(BF16), w_o_a[0].astype(BF16)
    d_a = wo_a.shape[0]
    bias_t, bias_step = _rel_bias_a(rel_bias_a[0], t_new)
    xp, ak_p, av_p = _mix_a_prompt(xp, wqkv_a, wo_a, bias_t, *ln(0, 1), bsz=bsz, seq=seq, tm=tm)
    xs, ak_s, av_s = _mix_a_step(xs, wqkv_a, wo_a, bias_step,
                                 feature_major(cache_a_k[0]), feature_major(cache_a_v[0]),
                                 *ln(0, 1), bsz=sbsz, t_new=t_new)
    xp, xs = ffn(xp, xs, 0, 1)

    xp, xs = ffn(xp, xs, 1, 0)
    wqkv_b, wo_b = w_qkv_b[0].astype(BF16), w_o_b[0].astype(BF16)
    dkv = (wqkv_b.shape[1] - wo_b.shape[0]) // 2
    n_kvblk = dkv // LANES
    heads_per_kvblk = sinks_b.shape[1] // n_kvblk
    keep_b = min((B_LEFT_CHUNKS) * CHUNK, seq)
    xp, bk_p, bv_p = _mix_b_prompt(
        xp, wqkv_b, wo_b, _rope_tables(jnp.arange(seq)), _pair_masks_b(heads_per_kvblk),
        jnp.transpose(_sink_column(sinks_b[0], n_kvblk, PAIR), (0, 2, 1)),
        *ln(1, 1), bsz=bsz, seq=seq, tm=tm, keep=keep_b)
    step_pos = jnp.tile(PAST_LEN + jnp.arange(t_new), sbsz)
    xs, bk_s, bv_s = _mix_b_step(
        xs, wqkv_b, wo_b, _rope_tables(step_pos), _sink_column(sinks_b[0], n_kvblk, t_new),
        feature_major(cache_b_k[0]), feature_major(cache_b_v[0]),
        *ln(1, 1), bsz=sbsz, t_new=t_new)
    xp, xs = ffn(xp, xs, 1, 1)

    heads_a = d_a // HEAD_DIM
    heads_kv = dkv // HEAD_DIM
    return (xp.reshape(bsz, seq, d), xs.reshape(sbsz, t_new, d),
            ak_p.reshape(1, bsz, -1, heads_a, HEAD_DIM), av_p.reshape(1, bsz, -1, heads_a, HEAD_DIM),
            bk_p.reshape(1, bsz, -1, heads_kv, HEAD_DIM), bv_p.reshape(1, bsz, -1, heads_kv, HEAD_DIM),
            ak_s.reshape(1, sbsz, t_new, heads_a, HEAD_DIM), av_s.reshape(1, sbsz, t_new, heads_a, HEAD_DIM),
            bk_s.reshape(1, sbsz, t_new, heads_kv, HEAD_DIM), bv_s.reshape(1, sbsz, t_new, heads_kv, HEAD_DIM))
```
